```python
import jax, jax.numpy as jnp
from jax import lax
import numpy as np

D_MODEL = 1024
BATCH = 8
SEQ = 2048
DEPTH = 4

GRID_W = 64
N_HEADS = 8
HEAD_DIM = 64
D_ATTN = N_HEADS * HEAD_DIM
KH_MAX = 8
KW = 16
Q_BLOCK = KW
K_BLOCK = 2 * KW
POOL_WINDOWS = (2, 4, 8, 16)
POOL_GROUPS = len(POOL_WINDOWS)
D_POOL = 512
POOL_GROUP_DIM = D_POOL // POOL_GROUPS
D_FF = 2816
CONV_W = 3
PLE_DIM = 256
N_PROJ = 3 * D_ATTN + D_POOL + 2 * D_MODEL
ALPHA = (2 * DEPTH) ** 0.25
BETA = (8 * DEPTH) ** -0.25
LN_EPS = 1e-5
NEG_INF = -1e30

kernel_name = "hybrid_natten_pool_convffn_encoder"


def layer_norm(x, g, b):
    xf = x.astype(jnp.float32)
    mu = jnp.mean(xf, axis=-1, keepdims=True)
    xc = xf - mu
    var = jnp.mean(xc * xc, axis=-1, keepdims=True)
    y = xc * lax.rsqrt(var + LN_EPS)
    return (y * g.astype(jnp.float32) + b.astype(jnp.float32)).astype(x.dtype)


def neighbourhood_attention(q, k, v, rpb):
    b, s, _ = q.shape
    rows = s // GRID_W
    kh = min(KH_MAX, rows)

    def to_grid(t):
        return t.reshape(b, rows, GRID_W, N_HEADS, HEAD_DIM).transpose(0, 3, 1, 2, 4)

    q, k, v = to_grid(q), to_grid(k), to_grid(v)
    r = np.arange(rows)
    row_start = np.clip(r - kh // 2, 0, rows - kh)
    row_idx = row_start[:, None] + np.arange(kh)
    k_rows = k[:, :, row_idx]
    v_rows = v[:, :, row_idx]
    row_off = row_idx - r[:, None] + KH_MAX - 1
    scale = HEAD_DIM ** -0.5
    outs = []
    for c0q in range(0, GRID_W, Q_BLOCK):
        c0k = int(np.clip(c0q - KW // 2, 0, GRID_W - K_BLOCK))
        qc = c0q + np.arange(Q_BLOCK)
        kc = c0k + np.arange(K_BLOCK)
        col_start = np.clip(qc - KW // 2, 0, GRID_W - KW)
        valid = (kc[None, :] >= col_start[:, None]) & (kc[None, :] < col_start[:, None] + KW)
        col_off = np.clip(kc[None, :] - qc[:, None], -(KW - 1), KW - 1) + KW - 1
        bias = rpb[:, row_off[:, None, :, None], col_off[None, :, None, :]]
        qb = q[:, :, :, c0q:c0q + Q_BLOCK]
        kb = k_rows[:, :, :, :, c0k:c0k + K_BLOCK]
        vb = v_rows[:, :, :, :, c0k:c0k + K_BLOCK]
        sc = jnp.einsum('bhrqd,bhrikd->bhrqik', qb, kb).astype(jnp.float32) * scale + bias.astype(jnp.float32)
        sc = jnp.where(valid[:, None, :], sc, NEG_INF)
        pr = jax.nn.softmax(sc.reshape(b, N_HEADS, rows, Q_BLOCK, kh * K_BLOCK), axis=-1)
        pr = pr.reshape(sc.shape).astype(v.dtype)
        outs.append(jnp.einsum('bhrqik,bhrikd->bhrqd', pr, vb))
    o = jnp.concatenate(outs, axis=3)
    return o.transpose(0, 2, 3, 1, 4).reshape(b, s, D_ATTN)


def multiscale_pool(u):
    b, s, _ = u.shape
    uf = u.astype(jnp.float32)
    cs = jnp.concatenate([jnp.zeros((b, 1, D_POOL), jnp.float32), jnp.cumsum(uf, axis=1)], axis=1)
    t = np.arange(s)
    outs = []
    for g, w in enumerate(POOL_WINDOWS):
        lo = np.clip(t - w // 2, 0, s)
        hi = np.clip(t + w // 2, 0, s)
        cnt = (hi - lo).astype(np.float32)
        sl = slice(g * POOL_GROUP_DIM, (g + 1) * POOL_GROUP_DIM)
        csg = cs[:, :, sl]
        mean = (csg[:, hi] - csg[:, lo]) / cnt[None, :, None]
        outs.append(mean - uf[:, :, sl])
    return jnp.concatenate(outs, axis=-1).astype(u.dtype)


def dwconv_centred(h, w, bias):
    s = h.shape[1]
    pad = CONV_W // 2
    hp = jnp.pad(h, ((0, 0), (pad, CONV_W - 1 - pad), (0, 0)))
    y = hp[:, 0:s] * w[0]
    for j in range(1, CONV_W):
        y = y + hp[:, j:j + s] * w[j]
    return y + bias


def setup_inputs(seed: int = 0) -> dict:
    key = jax.random.key(seed)
    ks = jax.random.split(key, 24)
    f32 = jnp.float32
    nrm = lambda k, shape, sc: jax.random.normal(k, shape, f32) * sc
    L, D = DEPTH, D_MODEL
    w_in = jnp.concatenate([
        nrm(ks[0], (L, D, D_ATTN), D ** -0.5),
        nrm(ks[1], (L, D, D_ATTN), D ** -0.5),
        nrm(ks[2], (L, D, D_ATTN), D ** -0.5 * BETA),
        nrm(ks[3], (L, D, D_POOL), D ** -0.5),
        nrm(ks[4], (L, D, 2 * D), D ** -0.5),
    ], axis=-1)
    return {
        "x": jax.random.normal(ks[5], (BATCH, SEQ, D), f32),
        "p": jax.random.normal(ks[6], (DEPTH, BATCH, SEQ, PLE_DIM), f32),
        "ln_in_g": 1.0 + nrm(ks[7], (D,), 0.02),
        "ln_in_b": nrm(ks[8], (D,), 0.02),
        "w_in": w_in,
        "b_in": nrm(ks[9], (L, N_PROJ), 0.01),
        "rpb": nrm(ks[10], (L, N_HEADS, 2 * KH_MAX - 1, 2 * KW - 1), 0.02),
        "w_attn_out": nrm(ks[11], (L, D_ATTN, D), D_ATTN ** -0.5 * BETA),
        "pool_w": nrm(ks[12], (L, POOL_GROUPS, POOL_GROUP_DIM, POOL_GROUP_DIM), POOL_GROUP_DIM ** -0.5),
        "pool_scale": 1.0 + nrm(ks[13], (L, D_POOL), 0.02),
        "w_pool_out": nrm(ks[14], (L, D_POOL, D), D_POOL ** -0.5 * BETA),
        "w_mix_out": nrm(ks[15], (L, D, D), D ** -0.5 * BETA),
        "ln1_g": 1.0 + nrm(ks[16], (L, D), 0.02),
        "ln1_b": nrm(ks[17], (L, D), 0.02),
        "w_up": nrm(ks[18], (L, D, 2 * D_FF), D ** -0.5),
        "conv_w": nrm(ks[19], (L, CONV_W, D_FF), CONV_W ** -0.5),
        "conv_b": nrm(ks[20], (L, D_FF), 0.01),
        "w_down": nrm(ks[21], (L, D_FF, D), D_FF ** -0.5 * BETA),
        "w_ple_gate": nrm(ks[22], (L, D, D), D ** -0.5),
        "w_ple_proj": nrm(ks[23], (L, PLE_DIM, D), PLE_DIM ** -0.5 * BETA),
        "ln2_g": 1.0 + nrm(jax.random.fold_in(key, 100), (L, D), 0.02),
        "ln2_b": nrm(jax.random.fold_in(key, 101), (L, D), 0.02),
    }


def reference(x, p, ln_in_g, ln_in_b, w_in, b_in, rpb, w_attn_out, pool_w, pool_scale,
              w_pool_out, w_mix_out, ln1_g, ln1_b, w_up, conv_w, conv_b, w_down,
              w_ple_gate, w_ple_proj, ln2_g, ln2_b):
    b, s, _ = x.shape
    splits = [D_ATTN, 2 * D_ATTN, 3 * D_ATTN, 3 * D_ATTN + D_POOL, 3 * D_ATTN + D_POOL + D_MODEL]
    h = layer_norm(x, ln_in_g, ln_in_b)
    for i in range(DEPTH):
        proj = h @ w_in[i] + b_in[i]
        q, k, v, u_pool, g_a, g_b = jnp.split(proj, splits, axis=-1)
        y_attn = neighbourhood_attention(q, k, v, rpb[i]) @ w_attn_out[i]
        pooled = multiscale_pool(u_pool).reshape(b, s, POOL_GROUPS, POOL_GROUP_DIM)
        pooled = jnp.einsum('bsgc,gcd->bsgd', pooled, pool_w[i]).reshape(b, s, D_POOL) * pool_scale[i]
        y_pool = pooled @ w_pool_out[i]
        merged = jax.nn.sigmoid(g_a) * y_attn + jax.nn.sigmoid(g_b) * y_pool
        h = layer_norm(ALPHA * h + merged @ w_mix_out[i], ln1_g[i], ln1_b[i])
        h_val, h_gate = jnp.split(h @ w_up[i], 2, axis=-1)
        act = jax.nn.gelu(dwconv_centred(h_gate, conv_w[i], conv_b[i]), approximate=False)
        ffn = (act * h_val) @ w_down[i]
        ple = jax.nn.sigmoid(h @ w_ple_gate[i]) * (p[i] @ w_ple_proj[i])
        h = layer_norm(ALPHA * h + ffn + ple, ln2_g[i], ln2_b[i])
    return h
```

```python
import functools

import numpy as np
import jax
import jax.numpy as jnp
from jax import lax
from jax.experimental import pallas as pl
from jax.experimental.pallas import tpu as pltpu

D_MODEL = 1024
BATCH = 8
SEQ = 2048
DEPTH = 4
GRID_W = 64
ROWS = SEQ // GRID_W
N_HEADS = 8
HEAD_DIM = 64
D_ATTN = N_HEADS * HEAD_DIM
N_PAIRS = N_HEADS // 2
KH = 8
KW = 16
WIN_KEYS = KH * GRID_W
POOL_WINDOWS = (2, 4, 8, 16)
D_POOL = 512
POOL_GROUP_DIM = 128
D_FF = 2816
FF_CHUNK = 256
N_FF_CHUNKS = D_FF // FF_CHUNK
PLE_DIM = 256
ALPHA = (2 * DEPTH) ** 0.25
LN_EPS = 1e-5
NEG_INF = -1e30
QK_SCALE = HEAD_DIM ** -0.5

LANES = 128
BF16_SUBLANES = 16
VMEM_LIMIT = 56 * 1024 * 1024

TOKENS = BATCH * SEQ
LN_TILE = 1024
QKVU_CHUNK = 512
POOL_CHUNK = 256
POOL_HALO = 16
MIX_ROWS = 8
MIX_TILE = MIX_ROWS * GRID_W
FFN_TILE = 512
FFN_HALO = BF16_SUBLANES

f32 = jnp.float32
bf16 = jnp.bfloat16


def _layer_norm(xf, g, b):
    mu = jnp.mean(xf, axis=-1, keepdims=True)
    xc = xf - mu
    var = jnp.mean(xc * xc, axis=-1, keepdims=True)
    return xc * lax.rsqrt(var + LN_EPS) * g + b


def _dot(a, b):
    return jnp.dot(a, b, preferred_element_type=f32)


def _ln_kernel(x_ref, g_ref, b_ref, o_ref):
    o_ref[...] = _layer_norm(x_ref[...], g_ref[...], b_ref[...])


def _input_ln(x2d, g, b):
    return pl.pallas_call(
        _ln_kernel,
        grid=(TOKENS // LN_TILE,),
        in_specs=[
            pl.BlockSpec((LN_TILE, D_MODEL), lambda i: (i, 0)),
            pl.BlockSpec((1, D_MODEL), lambda i: (0, 0)),
            pl.BlockSpec((1, D_MODEL), lambda i: (0, 0)),
        ],
        out_specs=pl.BlockSpec((LN_TILE, D_MODEL), lambda i: (i, 0)),
        out_shape=jax.ShapeDtypeStruct((TOKENS, D_MODEL), f32),
        compiler_params=pltpu.CompilerParams(
            dimension_semantics=("arbitrary",), vmem_limit_bytes=VMEM_LIMIT),
        name="input_ln",
    )(x2d, g, b)


def _qkvu_kernel(h_ref, w_ref, b_ref, pw_ref, ps_ref,
                 qlo_ref, qhi_ref, k_ref, v_ref, pooled_ref,
                 upad_sc, pd_sc):
    lane = lax.broadcasted_iota(jnp.int32, (QKVU_CHUNK, D_ATTN), 1)
    even_head = (lane % LANES) < HEAD_DIM

    zeros_halo = jnp.zeros((POOL_HALO, D_POOL), f32)
    upad_sc[0:POOL_HALO, :] = zeros_halo
    upad_sc[POOL_HALO + SEQ:POOL_HALO + SEQ + POOL_HALO, :] = zeros_halo

    for c in range(SEQ // QKVU_CHUNK):
        rows = slice(c * QKVU_CHUNK, (c + 1) * QKVU_CHUNK)
        hb = h_ref[rows, :].astype(bf16)
        q = (_dot(hb, w_ref[:, 0:D_ATTN]) + b_ref[:, 0:D_ATTN]) * QK_SCALE
        qlo_ref[rows, :] = jnp.where(even_head, q, 0.0).astype(bf16)
        qhi_ref[rows, :] = jnp.where(even_head, 0.0, q).astype(bf16)
        k = _dot(hb, w_ref[:, D_ATTN:2 * D_ATTN]) + b_ref[:, D_ATTN:2 * D_ATTN]
        k_ref[rows, :] = k.astype(bf16)
        v = _dot(hb, w_ref[:, 2 * D_ATTN:3 * D_ATTN]) + b_ref[:, 2 * D_ATTN:3 * D_ATTN]
        v_ref[rows, :] = v.astype(bf16)
        u = _dot(hb, w_ref[:, 3 * D_ATTN:3 * D_ATTN + D_POOL]) + b_ref[:, 3 * D_ATTN:3 * D_ATTN + D_POOL]
        upad_sc[POOL_HALO + c * QKVU_CHUNK:POOL_HALO + (c + 1) * QKVU_CHUNK, :] = u

    for c in range(SEQ // POOL_CHUNK):
        t = lax.broadcasted_iota(jnp.int32, (POOL_CHUNK, 1), 0) + c * POOL_CHUNK
        base = POOL_HALO + c * POOL_CHUNK
        for g, w in enumerate(POOL_WINDOWS):
            half = w // 2
            cols = slice(g * POOL_GROUP_DIM, (g + 1) * POOL_GROUP_DIM)
            acc = upad_sc[base - half:base - half + POOL_CHUNK, cols]
            for j in range(-half + 1, half):
                acc = acc + upad_sc[base + j:base + j + POOL_CHUNK, cols]
            cnt = (jnp.minimum(t + half, SEQ) - jnp.maximum(t - half, 0)).astype(f32)
            centre = upad_sc[base:base + POOL_CHUNK, cols]
            pd_sc[c * POOL_CHUNK:(c + 1) * POOL_CHUNK, cols] = (acc / cnt - centre).astype(bf16)

    for g in range(len(POOL_WINDOWS)):
        cols = slice(g * POOL_GROUP_DIM, (g + 1) * POOL_GROUP_DIM)
        pooled = _dot(pd_sc[:, cols], pw_ref[g]) * ps_ref[:, cols]
        pooled_ref[:, cols] = pooled.astype(bf16)


def _qkvu(h3d, w_qkvu, b_qkvu, pool_w, pool_scale):
    seq_spec = lambda width: pl.BlockSpec((None, SEQ, width), lambda b: (b, 0, 0))
    full = lambda shape: pl.BlockSpec(shape, lambda b: (0,) * len(shape))
    out_sds = jax.ShapeDtypeStruct((BATCH, SEQ, D_ATTN), bf16)
    return pl.pallas_call(
        _qkvu_kernel,
        grid=(BATCH,),
        in_specs=[
            seq_spec(D_MODEL),
            full((D_MODEL, 3 * D_ATTN + D_POOL)),
            full((1, 3 * D_ATTN + D_POOL)),
            full((len(POOL_WINDOWS), POOL_GROUP_DIM, POOL_GROUP_DIM)),
            full((1, D_POOL)),
        ],
        out_specs=[seq_spec(D_ATTN)] * 5,
        out_shape=[out_sds] * 5,
        scratch_shapes=[
            pltpu.VMEM((SEQ + 2 * POOL_HALO, D_POOL), f32),
            pltpu.VMEM((SEQ, D_POOL), bf16),
        ],
        compiler_params=pltpu.CompilerParams(
            dimension_semantics=("arbitrary",), vmem_limit_bytes=VMEM_LIMIT),
        name="qkvu",
    )(h3d, w_qkvu, b_qkvu, pool_w, pool_scale)


def _mix_kernel(h_ref, qlo_ref, qhi_ref, k_ref, v_ref, pooled_ref, bias_ref, valid_ref,
                wg_ref, bg_ref, wa_ref, wp_ref, wm_ref, g_ref, b_ref,
                o_ref, attn_sc):
    c = pl.program_id(1)
    even_head = lax.broadcasted_iota(jnp.int32, (GRID_W, LANES), 1) < HEAD_DIM

    def row_body(rr, carry):
        r = c * MIX_ROWS + rr
        rs = jnp.clip(r - KH // 2, 0, ROWS - KH)
        delta = r - rs
        q0 = pl.multiple_of(rr * GRID_W, GRID_W)
        k0 = pl.multiple_of(rs * GRID_W, GRID_W)
        for p in range(N_PAIRS):
            cols = slice(p * LANES, (p + 1) * LANES)
            lhs = jnp.concatenate(
                [qlo_ref[pl.ds(q0, GRID_W), cols], qhi_ref[pl.ds(q0, GRID_W), cols]], axis=0)
            kw = k_ref[pl.ds(k0, WIN_KEYS), cols]
            s = lax.dot_general(lhs, kw, (((1,), (1,)), ((), ())),
                                preferred_element_type=f32)
            s = jnp.where(valid_ref[...] > 0.5, s + bias_ref[delta, p], NEG_INF)
            m = jnp.max(s, axis=-1, keepdims=True)
            e = jnp.exp(s - m)
            l = jnp.sum(e, axis=-1, keepdims=True)
            vw = v_ref[pl.ds(k0, WIN_KEYS), cols]
            o = _dot(e.astype(bf16), vw) / l
            o = jnp.where(even_head, o[:GRID_W], o[GRID_W:])
            attn_sc[pl.ds(q0, GRID_W), cols] = o.astype(bf16)
        return carry

    lax.fori_loop(0, MIX_ROWS, row_body, 0)

    h = h_ref[...]
    hb = h.astype(bf16)
    ga = jax.nn.sigmoid(_dot(hb, wg_ref[:, 0:D_MODEL]) + bg_ref[:, 0:D_MODEL])
    merged = ga * _dot(attn_sc[...], wa_ref[...])
    gb = jax.nn.sigmoid(_dot(hb, wg_ref[:, D_MODEL:2 * D_MODEL]) + bg_ref[:, D_MODEL:2 * D_MODEL])
    merged = merged + gb * _dot(pooled_ref[...], wp_ref[...])
    z = ALPHA * h + _dot(merged.astype(bf16), wm_ref[...])
    o_ref[...] = _layer_norm(z, g_ref[...], b_ref[...])


def _mix(h3d, qlo, qhi, k, v, pooled, bias_tbl, valid_tbl,
         w_gate, b_gate, w_attn_out, w_pool_out, w_mix_out, ln_g, ln_b):
    tile = lambda width: pl.BlockSpec((None, MIX_TILE, width), lambda b, c: (b, c, 0))
    seq = lambda width: pl.BlockSpec((None, SEQ, width), lambda b, c: (b, 0, 0))
    full = lambda shape: pl.BlockSpec(shape, lambda b, c: (0,) * len(shape))
    return pl.pallas_call(
        _mix_kernel,
        grid=(BATCH, SEQ // MIX_TILE),
        in_specs=[
            tile(D_MODEL), tile(D_ATTN), tile(D_ATTN), seq(D_ATTN), seq(D_ATTN), tile(D_POOL),
            full((KH, N_PAIRS, 2 * GRID_W, WIN_KEYS)),
            full((2 * GRID_W, WIN_KEYS)),
            full((D_MODEL, 2 * D_MODEL)), full((1, 2 * D_MODEL)),
            full((D_ATTN, D_MODEL)), full((D_POOL, D_MODEL)), full((D_MODEL, D_MODEL)),
            full((1, D_MODEL)), full((1, D_MODEL)),
        ],
        out_specs=tile(D_MODEL),
        out_shape=jax.ShapeDtypeStruct((BATCH, SEQ, D_MODEL), f32),
        scratch_shapes=[pltpu.VMEM((MIX_TILE, D_ATTN), bf16)],
        compiler_params=pltpu.CompilerParams(
            dimension_semantics=("arbitrary", "arbitrary"), vmem_limit_bytes=VMEM_LIMIT),
        name="mix",
    )(h3d, qlo, qhi, k, v, pooled, bias_tbl, valid_tbl,
      w_gate, b_gate, w_attn_out, w_pool_out, w_mix_out, ln_g, ln_b)


def _ffn_kernel(h_ref, hprev_ref, hnext_ref, p_ref,
                wv_ref, wgt_ref, cw_ref, cb_ref, wd_ref, wpg_ref, wpp_ref, g_ref, b_ref,
                o_ref, hext_sc):
    i = pl.program_id(0)
    tiles_per_seq = SEQ // FFN_TILE
    pos = i % tiles_per_seq
    keep_prev = (pos != 0).astype(f32)
    keep_next = (pos != tiles_per_seq - 1).astype(f32)

    h = h_ref[...]
    hb = h.astype(bf16)
    main = slice(FFN_HALO, FFN_HALO + FFN_TILE)
    hext_sc[main, :] = hb
    hext_sc[0:FFN_HALO, :] = (hprev_ref[...] * keep_prev).astype(bf16)
    hext_sc[FFN_HALO + FFN_TILE:, :] = (hnext_ref[...] * keep_next).astype(bf16)

    ple = jax.nn.sigmoid(_dot(hb, wpg_ref[...])) * _dot(p_ref[...].astype(bf16), wpp_ref[...])
    acc = ALPHA * h + ple
    for c in range(N_FF_CHUNKS):
        val = _dot(hb, wv_ref[c])
        gate = _dot(hext_sc[...], wgt_ref[c])
        cw = cw_ref[c]
        y = (gate[FFN_HALO - 1:FFN_HALO - 1 + FFN_TILE] * cw[0:1]
             + gate[FFN_HALO:FFN_HALO + FFN_TILE] * cw[1:2]
             + gate[FFN_HALO + 1:FFN_HALO + 1 + FFN_TILE] * cw[2:3]
             + cb_ref[c])
        act = 0.5 * y * (1.0 + lax.erf(y * np.float32(np.sqrt(0.5))))
        acc = acc + _dot((act * val).astype(bf16), wd_ref[c])
    o_ref[...] = _layer_norm(acc, g_ref[...], b_ref[...])


def _ffn(h2d, p2d, w_val, w_gate, conv_w, conv_b, w_down, w_ple_gate, w_ple_proj, ln_g, ln_b):
    halo_blocks = FFN_TILE // FFN_HALO
    n_halo_blocks = TOKENS // FFN_HALO
    full = lambda shape: pl.BlockSpec(shape, lambda i: (0,) * len(shape))
    return pl.pallas_call(
        _ffn_kernel,
        grid=(TOKENS // FFN_TILE,),
        in_specs=[
            pl.BlockSpec((FFN_TILE, D_MODEL), lambda i: (i, 0)),
            pl.BlockSpec((FFN_HALO, D_MODEL), lambda i: (jnp.maximum(i * halo_blocks - 1, 0), 0)),
            pl.BlockSpec((FFN_HALO, D_MODEL),
                         lambda i: (jnp.minimum((i + 1) * halo_blocks, n_halo_blocks - 1), 0)),
            pl.BlockSpec((FFN_TILE, PLE_DIM), lambda i: (i, 0)),
            full((N_FF_CHUNKS, D_MODEL, FF_CHUNK)),
            full((N_FF_CHUNKS, D_MODEL, FF_CHUNK)),
            full((N_FF_CHUNKS, 3, FF_CHUNK)),
            full((N_FF_CHUNKS, 1, FF_CHUNK)),
            full((N_FF_CHUNKS, FF_CHUNK, D_MODEL)),
            full((D_MODEL, D_MODEL)),
            full((PLE_DIM, D_MODEL)),
            full((1, D_MODEL)), full((1, D_MODEL)),
        ],
        out_specs=pl.BlockSpec((FFN_TILE, D_MODEL), lambda i: (i, 0)),
        out_shape=jax.ShapeDtypeStruct((TOKENS, D_MODEL), f32),
        scratch_shapes=[pltpu.VMEM((FFN_TILE + 2 * FFN_HALO, D_MODEL), bf16)],
        compiler_params=pltpu.CompilerParams(
            dimension_semantics=("arbitrary",), vmem_limit_bytes=VMEM_LIMIT),
        name="ffn",
    )(h2d, h2d, h2d, p2d, w_val, w_gate, conv_w, conv_b, w_down, w_ple_gate, w_ple_proj, ln_g, ln_b)


def _column_geometry():
    qc = np.arange(GRID_W)[:, None]
    kc = np.arange(GRID_W)[None, :]
    col_start = np.clip(qc - KW // 2, 0, GRID_W - KW)
    valid = (kc >= col_start) & (kc < col_start + KW)
    col_off = np.clip(kc - qc, -(KW - 1), KW - 1) + KW - 1
    onehot = (col_off[..., None] == np.arange(2 * KW - 1)).astype(np.float32)
    valid_tbl = np.tile(valid.astype(np.float32), (2, KH))
    return onehot, valid_tbl


def _bias_tables(rpb):
    onehot, _ = _column_geometry()
    toeplitz = jnp.einsum('lhrc,qkc->lhrqk', rpb, jnp.asarray(onehot),
                          precision=lax.Precision.HIGHEST)
    tabs = []
    for delta in range(KH):
        t = toeplitz[:, :, KH - 1 - delta:2 * KH - 1 - delta]
        t = t.transpose(0, 1, 3, 2, 4).reshape(DEPTH, N_PAIRS, 2 * GRID_W, WIN_KEYS)
        tabs.append(t)
    return jnp.stack(tabs, axis=1)


def _ff_chunks_cols(w):
    return w.reshape(DEPTH, w.shape[1], N_FF_CHUNKS, FF_CHUNK).transpose(0, 2, 1, 3)


def kernel(x, p, ln_in_g, ln_in_b, w_in, b_in, rpb, w_attn_out, pool_w, pool_scale, w_pool_out,
           w_mix_out, ln1_g, ln1_b, w_up, conv_w, conv_b, w_down, w_ple_gate, w_ple_proj,
           ln2_g, ln2_b):
    n_qkvu = 3 * D_ATTN + D_POOL
    w_qkvu = w_in[:, :, :n_qkvu].astype(bf16)
    w_gate = w_in[:, :, n_qkvu:].astype(bf16)
    b_qkvu = b_in[:, None, :n_qkvu]
    b_gate = b_in[:, None, n_qkvu:]
    bias_tbl = _bias_tables(rpb)
    valid_tbl = jnp.asarray(_column_geometry()[1])
    w_attn_out_b = w_attn_out.astype(bf16)
    pool_w_b = pool_w.astype(bf16)
    w_pool_out_b = w_pool_out.astype(bf16)
    w_mix_out_b = w_mix_out.astype(bf16)
    w_val = _ff_chunks_cols(w_up[:, :, :D_FF]).astype(bf16)
    w_gt = _ff_chunks_cols(w_up[:, :, D_FF:]).astype(bf16)
    conv_w_c = conv_w.reshape(DEPTH, 3, N_FF_CHUNKS, FF_CHUNK).transpose(0, 2, 1, 3)
    conv_b_c = conv_b.reshape(DEPTH, N_FF_CHUNKS, 1, FF_CHUNK)
    w_down_c = w_down.reshape(DEPTH, N_FF_CHUNKS, FF_CHUNK, D_MODEL).astype(bf16)
    w_ple_gate_b = w_ple_gate.astype(bf16)
    w_ple_proj_b = w_ple_proj.astype(bf16)
    p3d = p.reshape(DEPTH, TOKENS, PLE_DIM)
    row = lambda a: a.reshape(1, -1)

    h = _input_ln(x.reshape(TOKENS, D_MODEL), row(ln_in_g), row(ln_in_b))
    for i in range(DEPTH):
        h3d = h.reshape(BATCH, SEQ, D_MODEL)
        qlo, qhi, k, v, pooled = _qkvu(h3d, w_qkvu[i], b_qkvu[i], pool_w_b[i], row(pool_scale[i]))
        h3d = _mix(h3d, qlo, qhi, k, v, pooled, bias_tbl[i], valid_tbl,
                   w_gate[i], b_gate[i], w_attn_out_b[i], w_pool_out_b[i], w_mix_out_b[i],
                   row(ln1_g[i]), row(ln1_b[i]))
        h = _ffn(h3d.reshape(TOKENS, D_MODEL), p3d[i], w_val[i], w_gt[i], conv_w_c[i], conv_b_c[i],
                 w_down_c[i], w_ple_gate_b[i], w_ple_proj_b[i], row(ln2_g[i]), row(ln2_b[i]))
    return h.reshape(BATCH, SEQ, D_MODEL)
```

```python
import functools

import numpy as np
import jax
import jax.numpy as jnp
from jax import lax
from jax.experimental import pallas as pl
from jax.experimental.pallas import tpu as pltpu

D_MODEL = 1024
BATCH = 8
SEQ = 2048
DEPTH = 4
GRID_W = 64
ROWS = SEQ // GRID_W
N_HEADS = 8
HEAD_DIM = 64
D_ATTN = N_HEADS * HEAD_DIM
N_PAIRS = N_HEADS // 2
KH = 8
KW = 16
WIN_KEYS = KH * GRID_W
POOL_WINDOWS = (2, 4, 8, 16)
D_POOL = 512
POOL_GROUP_DIM = 128
D_FF = 2816
FF_CHUNK = 256
N_FF_CHUNKS = D_FF // FF_CHUNK
PLE_DIM = 256
ALPHA = (2 * DEPTH) ** 0.25
LN_EPS = 1e-5
NEG_INF = -1e30
QK_SCALE = HEAD_DIM ** -0.5

LANES = 128
BF16_SUBLANES = 16
VMEM_LIMIT = 56 * 1024 * 1024

TOKENS = BATCH * SEQ
LN_TILE = 1024
QKVU_CHUNK = 512
POOL_CHUNK = 256
POOL_HALO = 16
MIX_ROWS = 8
MIX_TILE = MIX_ROWS * GRID_W
FFN_TILE = 512
FFN_HALO = BF16_SUBLANES

f32 = jnp.float32
bf16 = jnp.bfloat16


def _layer_norm(xf, g, b):
    mu = jnp.mean(xf, axis=-1, keepdims=True)
    xc = xf - mu
    var = jnp.mean(xc * xc, axis=-1, keepdims=True)
    return xc * lax.rsqrt(var + LN_EPS) * g + b


def _dot(a, b):
    return jnp.dot(a, b, preferred_element_type=f32)


def _ln_kernel(x_ref, g_ref, b_ref, o_ref):
    o_ref[...] = _layer_norm(x_ref[...], g_ref[...], b_ref[...])


def _input_ln(x2d, g, b):
    return pl.pallas_call(
        _ln_kernel,
        grid=(TOKENS // LN_TILE,),
        in_specs=[
            pl.BlockSpec((LN_TILE, D_MODEL), lambda i: (i, 0)),
            pl.BlockSpec((1, D_MODEL), lambda i: (0, 0)),
            pl.BlockSpec((1, D_MODEL), lambda i: (0, 0)),
        ],
        out_specs=pl.BlockSpec((LN_TILE, D_MODEL), lambda i: (i, 0)),
        out_shape=jax.ShapeDtypeStruct((TOKENS, D_MODEL), f32),
        compiler_params=pltpu.CompilerParams(
            dimension_semantics=("arbitrary",), vmem_limit_bytes=VMEM_LIMIT),
        name="input_ln",
    )(x2d, g, b)


def _qkvu_kernel(h_ref, w_ref, b_ref, pw_ref, ps_ref,
                 qlo_ref, qhi_ref, k_ref, v_ref, pooled_ref,
                 upad_sc, pd_sc):
    lane = lax.broadcasted_iota(jnp.int32, (QKVU_CHUNK, D_ATTN), 1)
    even_head = (lane % LANES) < HEAD_DIM

    zeros_halo = jnp.zeros((POOL_HALO, D_POOL), f32)
    upad_sc[0:POOL_HALO, :] = zeros_halo
    upad_sc[POOL_HALO + SEQ:POOL_HALO + SEQ + POOL_HALO, :] = zeros_halo

    for c in range(SEQ // QKVU_CHUNK):
        rows = slice(c * QKVU_CHUNK, (c + 1) * QKVU_CHUNK)
        hb = h_ref[rows, :].astype(bf16)
        q = (_dot(hb, w_ref[:, 0:D_ATTN]) + b_ref[:, 0:D_ATTN]) * QK_SCALE
        qlo_ref[rows, :] = jnp.where(even_head, q, 0.0).astype(bf16)
        qhi_ref[rows, :] = jnp.where(even_head, 0.0, q).astype(bf16)
        k = _dot(hb, w_ref[:, D_ATTN:2 * D_ATTN]) + b_ref[:, D_ATTN:2 * D_ATTN]
        k_ref[rows, :] = k.astype(bf16)
        v = _dot(hb, w_ref[:, 2 * D_ATTN:3 * D_ATTN]) + b_ref[:, 2 * D_ATTN:3 * D_ATTN]
        v_ref[rows, :] = v.astype(bf16)
        u = _dot(hb, w_ref[:, 3 * D_ATTN:3 * D_ATTN + D_POOL]) + b_ref[:, 3 * D_ATTN:3 * D_ATTN + D_POOL]
        upad_sc[POOL_HALO + c * QKVU_CHUNK:POOL_HALO + (c + 1) * QKVU_CHUNK, :] = u

    for c in range(SEQ // POOL_CHUNK):
        t = lax.broadcasted_iota(jnp.int32, (POOL_CHUNK, 1), 0) + c * POOL_CHUNK
        base = POOL_HALO + c * POOL_CHUNK
        for g, w in enumerate(POOL_WINDOWS):
            half = w // 2
            cols = slice(g * POOL_GROUP_DIM, (g + 1) * POOL_GROUP_DIM)
            acc = upad_sc[base - half:base - half + POOL_CHUNK, cols]
            for j in range(-half + 1, half):
                acc = acc + upad_sc[base + j:base + j + POOL_CHUNK, cols]
            cnt = (jnp.minimum(t + half, SEQ) - jnp.maximum(t - half, 0)).astype(f32)
            centre = upad_sc[base:base + POOL_CHUNK, cols]
            pd_sc[c * POOL_CHUNK:(c + 1) * POOL_CHUNK, cols] = (acc / cnt - centre).astype(bf16)

    for g in range(len(POOL_WINDOWS)):
        cols = slice(g * POOL_GROUP_DIM, (g + 1) * POOL_GROUP_DIM)
        pooled = _dot(pd_sc[:, cols], pw_ref[g]) * ps_ref[:, cols]
        pooled_ref[:, cols] = pooled.astype(bf16)


def _qkvu(h3d, w_qkvu, b_qkvu, pool_w, pool_scale):
    seq_spec = lambda width: pl.BlockSpec((None, SEQ, width), lambda b: (b, 0, 0))
    full = lambda shape: pl.BlockSpec(shape, lambda b: (0,) * len(shape))
    out_sds = jax.ShapeDtypeStruct((BATCH, SEQ, D_ATTN), bf16)
    return pl.pallas_call(
        _qkvu_kernel,
        grid=(BATCH,),
        in_specs=[
            seq_spec(D_MODEL),
            full((D_MODEL, 3 * D_ATTN + D_POOL)),
            full((1, 3 * D_ATTN + D_POOL)),
            full((len(POOL_WINDOWS), POOL_GROUP_DIM, POOL_GROUP_DIM)),
            full((1, D_POOL)),
        ],
        out_specs=[seq_spec(D_ATTN)] * 5,
        out_shape=[out_sds] * 5,
        scratch_shapes=[
            pltpu.VMEM((SEQ + 2 * POOL_HALO, D_POOL), f32),
            pltpu.VMEM((SEQ, D_POOL), bf16),
        ],
        compiler_params=pltpu.CompilerParams(
            dimension_semantics=("arbitrary",), vmem_limit_bytes=VMEM_LIMIT),
        name="qkvu",
    )(h3d, w_qkvu, b_qkvu, pool_w, pool_scale)


def _mix_kernel(h_ref, qlo_ref, qhi_ref, k_ref, v_ref, pooled_ref, bias_ref, valid_ref,
                wg_ref, bg_ref, wa_ref, wp_ref, wm_ref, g_ref, b_ref,
                o_ref, attn_sc, s_sc, e_sc, l_sc):
    c = pl.program_id(1)
    even_head = lax.broadcasted_iota(jnp.int32, (GRID_W, LANES), 1) < HEAD_DIM

    def window(rr):
        r = c * MIX_ROWS + rr
        rs = jnp.clip(r - KH // 2, 0, ROWS - KH)
        return r - rs, pl.multiple_of(rs * GRID_W, GRID_W)

    def scores(rr, slot):
        _, k0 = window(rr)
        rows = slice(rr * GRID_W, (rr + 1) * GRID_W)
        for p in range(N_PAIRS):
            cols = slice(p * LANES, (p + 1) * LANES)
            lhs = jnp.concatenate([qlo_ref[rows, cols], qhi_ref[rows, cols]], axis=0)
            kw = k_ref[pl.ds(k0, WIN_KEYS), cols]
            s_sc[slot, p] = lax.dot_general(lhs, kw, (((1,), (1,)), ((), ())),
                                            preferred_element_type=f32)

    def softmax(rr, slot):
        delta, _ = window(rr)
        valid = valid_ref[...] > 0.5
        for p in range(N_PAIRS):
            parts = []
            for j in range(WIN_KEYS // LANES):
                sj = s_sc[slot, p, :, j * LANES:(j + 1) * LANES] + bias_ref[2 * j + KH - 1 - delta, p]
                parts.append(jnp.where(valid, sj, NEG_INF))
            m = functools.reduce(jnp.maximum, parts)
            m = jnp.max(m, axis=-1, keepdims=True)
            es = [jnp.exp(sj - m) for sj in parts]
            l_sc[slot, p] = jnp.sum(functools.reduce(lambda x, y: x + y, es), axis=-1, keepdims=True)
            for j, ej in enumerate(es):
                e_sc[slot, p, :, j * LANES:(j + 1) * LANES] = ej.astype(bf16)

    def weighted_values(rr, slot):
        _, k0 = window(rr)
        rows = slice(rr * GRID_W, (rr + 1) * GRID_W)
        for p in range(N_PAIRS):
            cols = slice(p * LANES, (p + 1) * LANES)
            vw = v_ref[pl.ds(k0, WIN_KEYS), cols]
            o = _dot(e_sc[slot, p], vw) / l_sc[slot, p]
            o = jnp.where(even_head, o[:GRID_W], o[GRID_W:])
            attn_sc[rows, cols] = o.astype(bf16)

    for it in range(MIX_ROWS + 2):
        if it < MIX_ROWS:
            scores(it, it % 2)
        if 1 <= it <= MIX_ROWS:
            softmax(it - 1, (it - 1) % 2)
        if it >= 2:
            weighted_values(it - 2, it % 2)

    h = h_ref[...]
    hb = h.astype(bf16)
    ga = jax.nn.sigmoid(_dot(hb, wg_ref[:, 0:D_MODEL]) + bg_ref[:, 0:D_MODEL])
    merged = ga * _dot(attn_sc[...], wa_ref[...])
    gb = jax.nn.sigmoid(_dot(hb, wg_ref[:, D_MODEL:2 * D_MODEL]) + bg_ref[:, D_MODEL:2 * D_MODEL])
    merged = merged + gb * _dot(pooled_ref[...], wp_ref[...])
    z = ALPHA * h + _dot(merged.astype(bf16), wm_ref[...])
    o_ref[...] = _layer_norm(z, g_ref[...], b_ref[...])


def _mix(h3d, qlo, qhi, k, v, pooled, bias_tbl, valid_tbl,
         w_gate, b_gate, w_attn_out, w_pool_out, w_mix_out, ln_g, ln_b):
    tile = lambda width: pl.BlockSpec((None, MIX_TILE, width), lambda b, c: (b, c, 0))
    seq = lambda width: pl.BlockSpec((None, SEQ, width), lambda b, c: (b, 0, 0))
    full = lambda shape: pl.BlockSpec(shape, lambda b, c: (0,) * len(shape))
    return pl.pallas_call(
        _mix_kernel,
        grid=(BATCH, SEQ // MIX_TILE),
        in_specs=[
            tile(D_MODEL), tile(D_ATTN), tile(D_ATTN), seq(D_ATTN), seq(D_ATTN), tile(D_POOL),
            full((2 * KH - 2, N_PAIRS, 2 * GRID_W, LANES)),
            full((2 * GRID_W, LANES)),
            full((D_MODEL, 2 * D_MODEL)), full((1, 2 * D_MODEL)),
            full((D_ATTN, D_MODEL)), full((D_POOL, D_MODEL)), full((D_MODEL, D_MODEL)),
            full((1, D_MODEL)), full((1, D_MODEL)),
        ],
        out_specs=tile(D_MODEL),
        out_shape=jax.ShapeDtypeStruct((BATCH, SEQ, D_MODEL), f32),
        scratch_shapes=[
            pltpu.VMEM((MIX_TILE, D_ATTN), bf16),
            pltpu.VMEM((2, N_PAIRS, 2 * GRID_W, WIN_KEYS), f32),
            pltpu.VMEM((2, N_PAIRS, 2 * GRID_W, WIN_KEYS), bf16),
            pltpu.VMEM((2, N_PAIRS, 2 * GRID_W, 1), f32),
        ],
        compiler_params=pltpu.CompilerParams(
            dimension_semantics=("arbitrary", "arbitrary"), vmem_limit_bytes=VMEM_LIMIT),
        name="mix",
    )(h3d, qlo, qhi, k, v, pooled, bias_tbl, valid_tbl,
      w_gate, b_gate, w_attn_out, w_pool_out, w_mix_out, ln_g, ln_b)


def _ffn_kernel(layer_ref, h_ref, hprev_ref, hnext_ref, p_ref,
                wv_ref, wgt_ref, cw_ref, cb_ref, wd_ref, wpg_ref, wpp_ref, g_ref, b_ref,
                o_ref, hext_sc, gate_sc):
    del layer_ref
    i = pl.program_id(0)
    tiles_per_seq = SEQ // FFN_TILE
    pos = i % tiles_per_seq
    keep_prev = (pos != 0).astype(f32)
    keep_next = (pos != tiles_per_seq - 1).astype(f32)

    h = h_ref[...]
    hb = h.astype(bf16)
    main = slice(FFN_HALO, FFN_HALO + FFN_TILE)
    hext_sc[main, :] = hb
    hext_sc[0:FFN_HALO, :] = (hprev_ref[...] * keep_prev).astype(bf16)
    hext_sc[FFN_HALO + FFN_TILE:, :] = (hnext_ref[...] * keep_next).astype(bf16)

    ple = jax.nn.sigmoid(_dot(hb, wpg_ref[...])) * _dot(p_ref[...].astype(bf16), wpp_ref[...])
    acc = ALPHA * h + ple
    for c in range(N_FF_CHUNKS):
        cols = slice(c * FF_CHUNK, (c + 1) * FF_CHUNK)
        val = _dot(hb, wv_ref[:, cols])
        gate_sc[...] = _dot(hext_sc[...], wgt_ref[:, cols])
        y = (gate_sc[FFN_HALO - 1:FFN_HALO - 1 + FFN_TILE, :] * cw_ref[0:1, cols]
             + gate_sc[FFN_HALO:FFN_HALO + FFN_TILE, :] * cw_ref[1:2, cols]
             + gate_sc[FFN_HALO + 1:FFN_HALO + 1 + FFN_TILE, :] * cw_ref[2:3, cols]
             + cb_ref[:, cols])
        act = 0.5 * y * (1.0 + lax.erf(y * np.float32(np.sqrt(0.5))))
        acc = acc + _dot((act * val).astype(bf16), wd_ref[cols, :])
    o_ref[...] = _layer_norm(acc, g_ref[...], b_ref[...])


def _ffn(layer, h2d, p3d, w_val, w_gate, conv_w, conv_b, w_down, w_ple_gate, w_ple_proj, ln_g, ln_b):
    halo_blocks = FFN_TILE // FFN_HALO
    n_halo_blocks = TOKENS // FFN_HALO
    full = lambda shape: pl.BlockSpec(shape, lambda i, l: (0,) * len(shape))
    grid_spec = pltpu.PrefetchScalarGridSpec(
        num_scalar_prefetch=1,
        grid=(TOKENS // FFN_TILE,),
        in_specs=[
            pl.BlockSpec((FFN_TILE, D_MODEL), lambda i, l: (i, 0)),
            pl.BlockSpec((FFN_HALO, D_MODEL), lambda i, l: (jnp.maximum(i * halo_blocks - 1, 0), 0)),
            pl.BlockSpec((FFN_HALO, D_MODEL),
                         lambda i, l: (jnp.minimum((i + 1) * halo_blocks, n_halo_blocks - 1), 0)),
            pl.BlockSpec((None, FFN_TILE, PLE_DIM), lambda i, l: (l[0], i, 0)),
            full((D_MODEL, D_FF)),
            full((D_MODEL, D_FF)),
            full((3, D_FF)),
            full((1, D_FF)),
            full((D_FF, D_MODEL)),
            full((D_MODEL, D_MODEL)),
            full((PLE_DIM, D_MODEL)),
            full((1, D_MODEL)), full((1, D_MODEL)),
        ],
        out_specs=pl.BlockSpec((FFN_TILE, D_MODEL), lambda i, l: (i, 0)),
        scratch_shapes=[
            pltpu.VMEM((FFN_TILE + 2 * FFN_HALO, D_MODEL), bf16),
            pltpu.VMEM((FFN_TILE + 2 * FFN_HALO, FF_CHUNK), f32),
        ],
    )
    return pl.pallas_call(
        _ffn_kernel,
        grid_spec=grid_spec,
        out_shape=jax.ShapeDtypeStruct((TOKENS, D_MODEL), f32),
        compiler_params=pltpu.CompilerParams(
            dimension_semantics=("arbitrary",), vmem_limit_bytes=VMEM_LIMIT),
        name="ffn",
    )(layer, h2d, h2d, h2d, p3d, w_val, w_gate, conv_w, conv_b, w_down, w_ple_gate, w_ple_proj, ln_g, ln_b)


def _column_geometry():
    qc = np.arange(GRID_W)[:, None]
    kc = np.arange(GRID_W)[None, :]
    col_start = np.clip(qc - KW // 2, 0, GRID_W - KW)
    valid = (kc >= col_start) & (kc < col_start + KW)
    col_off = np.clip(kc - qc, -(KW - 1), KW - 1) + KW - 1
    onehot = (col_off[..., None] == np.arange(2 * KW - 1)).astype(np.float32)
    valid_tbl = np.tile(valid.astype(np.float32), (2, 2))
    return onehot, valid_tbl


def _bias_table(rpb):
    onehot, _ = _column_geometry()
    toeplitz = jnp.einsum('hrc,qkc->hrqk', rpb, jnp.asarray(onehot),
                          precision=lax.Precision.HIGHEST)
    two_rows = jnp.stack([toeplitz[:, :-1], toeplitz[:, 1:]], axis=3)
    return two_rows.transpose(1, 0, 2, 3, 4).reshape(2 * KH - 2, N_PAIRS, 2 * GRID_W, LANES)


def kernel(x, p, ln_in_g, ln_in_b, w_in, b_in, rpb, w_attn_out, pool_w, pool_scale, w_pool_out,
           w_mix_out, ln1_g, ln1_b, w_up, conv_w, conv_b, w_down, w_ple_gate, w_ple_proj,
           ln2_g, ln2_b):
    n_qkvu = 3 * D_ATTN + D_POOL
    valid_tbl = jnp.asarray(_column_geometry()[1])
    p3d = p.reshape(DEPTH, TOKENS, PLE_DIM)
    row = lambda a: a.reshape(1, -1)

    h = _input_ln(x.reshape(TOKENS, D_MODEL), row(ln_in_g), row(ln_in_b))
    for i in range(DEPTH):
        h3d = h.reshape(BATCH, SEQ, D_MODEL)
        qlo, qhi, k, v, pooled = _qkvu(
            h3d, w_in[i, :, :n_qkvu].astype(bf16), row(b_in[i, :n_qkvu]),
            pool_w[i].astype(bf16), row(pool_scale[i]))
        h3d = _mix(h3d, qlo, qhi, k, v, pooled, _bias_table(rpb[i]), valid_tbl,
                   w_in[i, :, n_qkvu:].astype(bf16), row(b_in[i, n_qkvu:]),
                   w_attn_out[i].astype(bf16), w_pool_out[i].astype(bf16), w_mix_out[i].astype(bf16),
                   row(ln1_g[i]), row(ln1_b[i]))
        h = _ffn(jnp.full((1,), i, jnp.int32), h3d.reshape(TOKENS, D_MODEL), p3d,
                 w_up[i, :, :D_FF].astype(bf16), w_up[i, :, D_FF:].astype(bf16),
                 conv_w[i], row(conv_b[i]), w_down[i].astype(bf16),
                 w_ple_gate[i].astype(bf16), w_ple_proj[i].astype(bf16),
                 row(ln2_g[i]), row(ln2_b[i]))
    return h.reshape(BATCH, SEQ, D_MODEL)
```

```python
import functools

import numpy as np
import jax
import jax.numpy as jnp
from jax import lax
from jax.experimental import pallas as pl
from jax.experimental.pallas import tpu as pltpu

D_MODEL = 1024
BATCH = 8
SEQ = 2048
DEPTH = 4
GRID_W = 64
ROWS = SEQ // GRID_W
N_HEADS = 8
HEAD_DIM = 64
D_ATTN = N_HEADS * HEAD_DIM
N_PAIRS = N_HEADS // 2
KH = 8
KW = 16
WIN_KEYS = KH * GRID_W
POOL_WINDOWS = (2, 4, 8, 16)
D_POOL = 512
POOL_GROUP_DIM = 128
D_FF = 2816
FF_CHUNK = 256
N_FF_CHUNKS = D_FF // FF_CHUNK
PLE_DIM = 256
ALPHA = (2 * DEPTH) ** 0.25
LN_EPS = 1e-5
NEG_INF = -1e30
QK_SCALE = HEAD_DIM ** -0.5

LANES = 128
BF16_SUBLANES = 16
VMEM_LIMIT = 56 * 1024 * 1024

TOKENS = BATCH * SEQ
LN_TILE = 1024
QKVU_CHUNK = 512
POOL_CHUNK = 256
POOL_HALO = 16
MIX_ROWS = 8
MIX_TILE = MIX_ROWS * GRID_W
GATE_BLOCK = 256
POOL_OUT_BLOCK = 512
FFN_TILE = 512
FFN_HALO = BF16_SUBLANES

f32 = jnp.float32
bf16 = jnp.bfloat16


def _layer_norm(xf, g, b):
    mu = jnp.mean(xf, axis=-1, keepdims=True)
    xc = xf - mu
    var = jnp.mean(xc * xc, axis=-1, keepdims=True)
    return xc * lax.rsqrt(var + LN_EPS) * g + b


def _dot(a, b):
    return jnp.dot(a, b, preferred_element_type=f32)


def _ln_kernel(x_ref, g_ref, b_ref, o_ref):
    o_ref[...] = _layer_norm(x_ref[...], g_ref[...], b_ref[...])


def _input_ln(x2d, g, b):
    return pl.pallas_call(
        _ln_kernel,
        grid=(TOKENS // LN_TILE,),
        in_specs=[
            pl.BlockSpec((LN_TILE, D_MODEL), lambda i: (i, 0)),
            pl.BlockSpec((1, D_MODEL), lambda i: (0, 0)),
            pl.BlockSpec((1, D_MODEL), lambda i: (0, 0)),
        ],
        out_specs=pl.BlockSpec((LN_TILE, D_MODEL), lambda i: (i, 0)),
        out_shape=jax.ShapeDtypeStruct((TOKENS, D_MODEL), f32),
        compiler_params=pltpu.CompilerParams(
            dimension_semantics=("arbitrary",), vmem_limit_bytes=VMEM_LIMIT),
        name="input_ln",
    )(x2d, g, b)


def _qkvu_kernel(h_ref, w_ref, b_ref, pw_ref, ps_ref,
                 qlo_ref, qhi_ref, k_ref, v_ref, pooled_ref,
                 upad_sc, pd_sc):
    lane = lax.broadcasted_iota(jnp.int32, (QKVU_CHUNK, D_ATTN), 1)
    even_head = (lane % LANES) < HEAD_DIM

    zeros_halo = jnp.zeros((POOL_HALO, D_POOL), f32)
    upad_sc[0:POOL_HALO, :] = zeros_halo
    upad_sc[POOL_HALO + SEQ:POOL_HALO + SEQ + POOL_HALO, :] = zeros_halo

    for c in range(SEQ // QKVU_CHUNK):
        rows = slice(c * QKVU_CHUNK, (c + 1) * QKVU_CHUNK)
        hb = h_ref[rows, :].astype(bf16)
        q = (_dot(hb, w_ref[:, 0:D_ATTN]) + b_ref[:, 0:D_ATTN]) * QK_SCALE
        qlo_ref[rows, :] = jnp.where(even_head, q, 0.0).astype(bf16)
        qhi_ref[rows, :] = jnp.where(even_head, 0.0, q).astype(bf16)
        k = _dot(hb, w_ref[:, D_ATTN:2 * D_ATTN]) + b_ref[:, D_ATTN:2 * D_ATTN]
        k_ref[rows, :] = k.astype(bf16)
        v = _dot(hb, w_ref[:, 2 * D_ATTN:3 * D_ATTN]) + b_ref[:, 2 * D_ATTN:3 * D_ATTN]
        v_ref[rows, :] = v.astype(bf16)
        u = _dot(hb, w_ref[:, 3 * D_ATTN:3 * D_ATTN + D_POOL]) + b_ref[:, 3 * D_ATTN:3 * D_ATTN + D_POOL]
        upad_sc[POOL_HALO + c * QKVU_CHUNK:POOL_HALO + (c + 1) * QKVU_CHUNK, :] = u

    for c in range(SEQ // POOL_CHUNK):
        t = lax.broadcasted_iota(jnp.int32, (POOL_CHUNK, 1), 0) + c * POOL_CHUNK
        base = POOL_HALO + c * POOL_CHUNK
        for g, w in enumerate(POOL_WINDOWS):
            half = w // 2
            cols = slice(g * POOL_GROUP_DIM, (g + 1) * POOL_GROUP_DIM)
            acc = upad_sc[base - half:base - half + POOL_CHUNK, cols]
            for j in range(-half + 1, half):
                acc = acc + upad_sc[base + j:base + j + POOL_CHUNK, cols]
            cnt = (jnp.minimum(t + half, SEQ) - jnp.maximum(t - half, 0)).astype(f32)
            centre = upad_sc[base:base + POOL_CHUNK, cols]
            pd_sc[c * POOL_CHUNK:(c + 1) * POOL_CHUNK, cols] = (acc / cnt - centre).astype(bf16)

    for g in range(len(POOL_WINDOWS)):
        cols = slice(g * POOL_GROUP_DIM, (g + 1) * POOL_GROUP_DIM)
        pooled = _dot(pd_sc[:, cols], pw_ref[g]) * ps_ref[:, cols]
        pooled_ref[:, cols] = pooled.astype(bf16)


def _qkvu(h3d, w_qkvu, b_qkvu, pool_w, pool_scale):
    seq_spec = lambda width: pl.BlockSpec((None, SEQ, width), lambda b: (b, 0, 0))
    full = lambda shape: pl.BlockSpec(shape, lambda b: (0,) * len(shape))
    out_sds = jax.ShapeDtypeStruct((BATCH, SEQ, D_ATTN), bf16)
    return pl.pallas_call(
        _qkvu_kernel,
        grid=(BATCH,),
        in_specs=[
            seq_spec(D_MODEL),
            full((D_MODEL, 3 * D_ATTN + D_POOL)),
            full((1, 3 * D_ATTN + D_POOL)),
            full((len(POOL_WINDOWS), POOL_GROUP_DIM, POOL_GROUP_DIM)),
            full((1, D_POOL)),
        ],
        out_specs=[seq_spec(D_ATTN)] * 5,
        out_shape=[out_sds] * 5,
        scratch_shapes=[
            pltpu.VMEM((SEQ + 2 * POOL_HALO, D_POOL), f32),
            pltpu.VMEM((SEQ, D_POOL), bf16),
        ],
        compiler_params=pltpu.CompilerParams(
            dimension_semantics=("arbitrary",), vmem_limit_bytes=VMEM_LIMIT),
        name="qkvu",
    )(h3d, w_qkvu, b_qkvu, pool_w, pool_scale)


def _mix_kernel(h_ref, qlo_ref, qhi_ref, k_ref, v_ref, pooled_ref, bias_ref, valid_ref,
                wg_ref, bg_ref, wa_ref, wp_ref, wm_ref, g_ref, b_ref,
                o_ref, attn_sc, s_sc, e_sc, l_sc, hb_sc, gsig_sc, ypool_sc):
    c = pl.program_id(1)
    even_head = lax.broadcasted_iota(jnp.int32, (GRID_W, LANES), 1) < HEAD_DIM

    def window(rr):
        r = c * MIX_ROWS + rr
        rs = jnp.clip(r - KH // 2, 0, ROWS - KH)
        return r - rs, pl.multiple_of(rs * GRID_W, GRID_W)

    def scores(rr, slot):
        _, k0 = window(rr)
        rows = slice(rr * GRID_W, (rr + 1) * GRID_W)
        for p in range(N_PAIRS):
            cols = slice(p * LANES, (p + 1) * LANES)
            lhs = jnp.concatenate([qlo_ref[rows, cols], qhi_ref[rows, cols]], axis=0)
            kw = k_ref[pl.ds(k0, WIN_KEYS), cols]
            s_sc[slot, p] = lax.dot_general(lhs, kw, (((1,), (1,)), ((), ())),
                                            preferred_element_type=f32)

    def softmax(rr, slot):
        delta, _ = window(rr)
        valid = valid_ref[...] > 0.5
        for p in range(N_PAIRS):
            parts = []
            for j in range(WIN_KEYS // LANES):
                sj = s_sc[slot, p, :, j * LANES:(j + 1) * LANES] + bias_ref[2 * j + KH - 1 - delta, p]
                parts.append(jnp.where(valid, sj, NEG_INF))
            m = functools.reduce(jnp.maximum, parts)
            m = jnp.max(m, axis=-1, keepdims=True)
            es = [jnp.exp(sj - m) for sj in parts]
            l_sc[slot, p] = jnp.sum(functools.reduce(lambda x, y: x + y, es), axis=-1, keepdims=True)
            for j, ej in enumerate(es):
                e_sc[slot, p, :, j * LANES:(j + 1) * LANES] = ej.astype(bf16)

    def weighted_values(rr, slot):
        _, k0 = window(rr)
        rows = slice(rr * GRID_W, (rr + 1) * GRID_W)
        for p in range(N_PAIRS):
            cols = slice(p * LANES, (p + 1) * LANES)
            vw = v_ref[pl.ds(k0, WIN_KEYS), cols]
            o = _dot(e_sc[slot, p], vw) / l_sc[slot, p]
            o = jnp.where(even_head, o[:GRID_W], o[GRID_W:])
            attn_sc[rows, cols] = o.astype(bf16)

    hb_sc[...] = h_ref[...].astype(bf16)

    def gate_block(j):
        cols = slice(j * GATE_BLOCK, (j + 1) * GATE_BLOCK)
        gsig_sc[:, cols] = jax.nn.sigmoid(_dot(hb_sc[...], wg_ref[:, cols]) + bg_ref[:, cols])

    def pool_block(j):
        cols = slice(j * POOL_OUT_BLOCK, (j + 1) * POOL_OUT_BLOCK)
        ypool_sc[:, cols] = _dot(pooled_ref[...], wp_ref[:, cols])

    n_gate_blocks = 2 * D_MODEL // GATE_BLOCK
    for it in range(MIX_ROWS + 2):
        if it < MIX_ROWS:
            scores(it, it % 2)
        if it < n_gate_blocks:
            gate_block(it)
        else:
            pool_block(it - n_gate_blocks)
        if 1 <= it <= MIX_ROWS:
            softmax(it - 1, (it - 1) % 2)
        if it >= 2:
            weighted_values(it - 2, it % 2)

    merged = (gsig_sc[:, 0:D_MODEL] * _dot(attn_sc[...], wa_ref[...])
              + gsig_sc[:, D_MODEL:2 * D_MODEL] * ypool_sc[...])
    z = ALPHA * h_ref[...] + _dot(merged.astype(bf16), wm_ref[...])
    o_ref[...] = _layer_norm(z, g_ref[...], b_ref[...])


def _mix(h3d, qlo, qhi, k, v, pooled, bias_tbl, valid_tbl,
         w_gate, b_gate, w_attn_out, w_pool_out, w_mix_out, ln_g, ln_b):
    tile = lambda width: pl.BlockSpec((None, MIX_TILE, width), lambda b, c: (b, c, 0))
    seq = lambda width: pl.BlockSpec((None, SEQ, width), lambda b, c: (b, 0, 0))
    full = lambda shape: pl.BlockSpec(shape, lambda b, c: (0,) * len(shape))
    return pl.pallas_call(
        _mix_kernel,
        grid=(BATCH, SEQ // MIX_TILE),
        in_specs=[
            tile(D_MODEL), tile(D_ATTN), tile(D_ATTN), seq(D_ATTN), seq(D_ATTN), tile(D_POOL),
            full((2 * KH - 2, N_PAIRS, 2 * GRID_W, LANES)),
            full((2 * GRID_W, LANES)),
            full((D_MODEL, 2 * D_MODEL)), full((1, 2 * D_MODEL)),
            full((D_ATTN, D_MODEL)), full((D_POOL, D_MODEL)), full((D_MODEL, D_MODEL)),
            full((1, D_MODEL)), full((1, D_MODEL)),
        ],
        out_specs=tile(D_MODEL),
        out_shape=jax.ShapeDtypeStruct((BATCH, SEQ, D_MODEL), f32),
        scratch_shapes=[
            pltpu.VMEM((MIX_TILE, D_ATTN), bf16),
            pltpu.VMEM((2, N_PAIRS, 2 * GRID_W, WIN_KEYS), f32),
            pltpu.VMEM((2, N_PAIRS, 2 * GRID_W, WIN_KEYS), bf16),
            pltpu.VMEM((2, N_PAIRS, 2 * GRID_W, 1), f32),
            pltpu.VMEM((MIX_TILE, D_MODEL), bf16),
            pltpu.VMEM((MIX_TILE, 2 * D_MODEL), f32),
            pltpu.VMEM((MIX_TILE, D_MODEL), f32),
        ],
        compiler_params=pltpu.CompilerParams(
            dimension_semantics=("arbitrary", "arbitrary"), vmem_limit_bytes=VMEM_LIMIT),
        name="mix",
    )(h3d, qlo, qhi, k, v, pooled, bias_tbl, valid_tbl,
      w_gate, b_gate, w_attn_out, w_pool_out, w_mix_out, ln_g, ln_b)


def _ffn_kernel(layer_ref, h_ref, hprev_ref, hnext_ref, p_ref,
                wv_ref, wgt_ref, cw_ref, cb_ref, wd_ref, wpg_ref, wpp_ref, g_ref, b_ref,
                o_ref, hext_sc, val_sc, gate_sc, act_sc, acc_sc):
    del layer_ref
    i = pl.program_id(0)
    tiles_per_seq = SEQ // FFN_TILE
    pos = i % tiles_per_seq
    keep_prev = (pos != 0).astype(f32)
    keep_next = (pos != tiles_per_seq - 1).astype(f32)

    h = h_ref[...]
    hb = h.astype(bf16)
    main = slice(FFN_HALO, FFN_HALO + FFN_TILE)
    hext_sc[main, :] = hb
    hext_sc[0:FFN_HALO, :] = (hprev_ref[...] * keep_prev).astype(bf16)
    hext_sc[FFN_HALO + FFN_TILE:, :] = (hnext_ref[...] * keep_next).astype(bf16)

    ple = jax.nn.sigmoid(_dot(hb, wpg_ref[...])) * _dot(p_ref[...].astype(bf16), wpp_ref[...])
    acc_sc[...] = ALPHA * h + ple

    def up(c, slot):
        cols = slice(c * FF_CHUNK, (c + 1) * FF_CHUNK)
        val_sc[slot] = _dot(hext_sc[main, :], wv_ref[:, cols])
        gate_sc[slot] = _dot(hext_sc[...], wgt_ref[:, cols])

    def activate(c, slot):
        cols = slice(c * FF_CHUNK, (c + 1) * FF_CHUNK)
        y = (gate_sc[slot, FFN_HALO - 1:FFN_HALO - 1 + FFN_TILE, :] * cw_ref[0:1, cols]
             + gate_sc[slot, FFN_HALO:FFN_HALO + FFN_TILE, :] * cw_ref[1:2, cols]
             + gate_sc[slot, FFN_HALO + 1:FFN_HALO + 1 + FFN_TILE, :] * cw_ref[2:3, cols]
             + cb_ref[:, cols])
        act = 0.5 * y * (1.0 + lax.erf(y * np.float32(np.sqrt(0.5))))
        act_sc[slot] = (act * val_sc[slot]).astype(bf16)

    def down(c, slot):
        acc_sc[...] += _dot(act_sc[slot], wd_ref[c * FF_CHUNK:(c + 1) * FF_CHUNK, :])

    for it in range(N_FF_CHUNKS + 2):
        if it < N_FF_CHUNKS:
            up(it, it % 2)
        if 1 <= it <= N_FF_CHUNKS:
            activate(it - 1, (it - 1) % 2)
        if it >= 2:
            down(it - 2, it % 2)
    o_ref[...] = _layer_norm(acc_sc[...], g_ref[...], b_ref[...])


def _ffn(layer, h2d, p3d, w_val, w_gate, conv_w, conv_b, w_down, w_ple_gate, w_ple_proj, ln_g, ln_b):
    halo_blocks = FFN_TILE // FFN_HALO
    n_halo_blocks = TOKENS // FFN_HALO
    full = lambda shape: pl.BlockSpec(shape, lambda i, l: (0,) * len(shape))
    grid_spec = pltpu.PrefetchScalarGridSpec(
        num_scalar_prefetch=1,
        grid=(TOKENS // FFN_TILE,),
        in_specs=[
            pl.BlockSpec((FFN_TILE, D_MODEL), lambda i, l: (i, 0)),
            pl.BlockSpec((FFN_HALO, D_MODEL), lambda i, l: (jnp.maximum(i * halo_blocks - 1, 0), 0)),
            pl.BlockSpec((FFN_HALO, D_MODEL),
                         lambda i, l: (jnp.minimum((i + 1) * halo_blocks, n_halo_blocks - 1), 0)),
            pl.BlockSpec((None, FFN_TILE, PLE_DIM), lambda i, l: (l[0], i, 0)),
            full((D_MODEL, D_FF)),
            full((D_MODEL, D_FF)),
            full((3, D_FF)),
            full((1, D_FF)),
            full((D_FF, D_MODEL)),
            full((D_MODEL, D_MODEL)),
            full((PLE_DIM, D_MODEL)),
            full((1, D_MODEL)), full((1, D_MODEL)),
        ],
        out_specs=pl.BlockSpec((FFN_TILE, D_MODEL), lambda i, l: (i, 0)),
        scratch_shapes=[
            pltpu.VMEM((FFN_TILE + 2 * FFN_HALO, D_MODEL), bf16),
            pltpu.VMEM((2, FFN_TILE, FF_CHUNK), f32),
            pltpu.VMEM((2, FFN_TILE + 2 * FFN_HALO, FF_CHUNK), f32),
            pltpu.VMEM((2, FFN_TILE, FF_CHUNK), bf16),
            pltpu.VMEM((FFN_TILE, D_MODEL), f32),
        ],
    )
    return pl.pallas_call(
        _ffn_kernel,
        grid_spec=grid_spec,
        out_shape=jax.ShapeDtypeStruct((TOKENS, D_MODEL), f32),
        compiler_params=pltpu.CompilerParams(
            dimension_semantics=("arbitrary",), vmem_limit_bytes=VMEM_LIMIT),
        name="ffn",
    )(layer, h2d, h2d, h2d, p3d, w_val, w_gate, conv_w, conv_b, w_down, w_ple_gate, w_ple_proj, ln_g, ln_b)


def _column_geometry():
    qc = np.arange(GRID_W)[:, None]
    kc = np.arange(GRID_W)[None, :]
    col_start = np.clip(qc - KW // 2, 0, GRID_W - KW)
    valid = (kc >= col_start) & (kc < col_start + KW)
    col_off = np.clip(kc - qc, -(KW - 1), KW - 1) + KW - 1
    onehot = (col_off[..., None] == np.arange(2 * KW - 1)).astype(np.float32)
    valid_tbl = np.tile(valid.astype(np.float32), (2, 2))
    return onehot, valid_tbl


def _bias_tables(rpb):
    onehot, _ = _column_geometry()
    select = np.einsum('ij,qkc->qikjc', np.eye(2, dtype=np.float32), onehot)
    two_rows = jnp.stack([rpb[:, :, :-1], rpb[:, :, 1:]], axis=3)
    tbl = jnp.einsum('lhrjc,qikjc->lrhqik', two_rows, jnp.asarray(select),
                     precision=lax.Precision.HIGHEST)
    return tbl.reshape(DEPTH, 2 * KH - 2, N_PAIRS, 2 * GRID_W, LANES)


def kernel(x, p, ln_in_g, ln_in_b, w_in, b_in, rpb, w_attn_out, pool_w, pool_scale, w_pool_out,
           w_mix_out, ln1_g, ln1_b, w_up, conv_w, conv_b, w_down, w_ple_gate, w_ple_proj,
           ln2_g, ln2_b):
    n_qkvu = 3 * D_ATTN + D_POOL
    valid_tbl = jnp.asarray(_column_geometry()[1])
    bias_tbl = _bias_tables(rpb)
    p3d = p.reshape(DEPTH, TOKENS, PLE_DIM)
    row = lambda a: a.reshape(1, -1)

    h = _input_ln(x.reshape(TOKENS, D_MODEL), row(ln_in_g), row(ln_in_b))
    for i in range(DEPTH):
        h3d = h.reshape(BATCH, SEQ, D_MODEL)
        qlo, qhi, k, v, pooled = _qkvu(
            h3d, w_in[i, :, :n_qkvu].astype(bf16), row(b_in[i, :n_qkvu]),
            pool_w[i].astype(bf16), row(pool_scale[i]))
        h3d = _mix(h3d, qlo, qhi, k, v, pooled, bias_tbl[i], valid_tbl,
                   w_in[i, :, n_qkvu:].astype(bf16), row(b_in[i, n_qkvu:]),
                   w_attn_out[i].astype(bf16), w_pool_out[i].astype(bf16), w_mix_out[i].astype(bf16),
                   row(ln1_g[i]), row(ln1_b[i]))
        h = _ffn(jnp.full((1,), i, jnp.int32), h3d.reshape(TOKENS, D_MODEL), p3d,
                 w_up[i, :, :D_FF].astype(bf16), w_up[i, :, D_FF:].astype(bf16),
                 conv_w[i], row(conv_b[i]), w_down[i].astype(bf16),
                 w_ple_gate[i].astype(bf16), w_ple_proj[i].astype(bf16),
                 row(ln2_g[i]), row(ln2_b[i]))
    return h.reshape(BATCH, SEQ, D_MODEL)
```

```python
import functools

import numpy as np
import jax
import jax.numpy as jnp
from jax import lax
from jax.experimental import pallas as pl
from jax.experimental.pallas import tpu as pltpu

D_MODEL = 1024
BATCH = 8
SEQ = 2048
DEPTH = 4
GRID_W = 64
ROWS = SEQ // GRID_W
N_HEADS = 8
HEAD_DIM = 64
D_ATTN = N_HEADS * HEAD_DIM
N_PAIRS = N_HEADS // 2
KH = 8
KW = 16
WIN_KEYS = KH * GRID_W
POOL_WINDOWS = (2, 4, 8, 16)
D_POOL = 512
POOL_GROUP_DIM = 128
D_FF = 2816
FF_CHUNK = 256
N_FF_CHUNKS = D_FF // FF_CHUNK
PLE_DIM = 256
ALPHA = (2 * DEPTH) ** 0.25
LN_EPS = 1e-5
NEG_INF = -1e30
QK_SCALE = HEAD_DIM ** -0.5

LANES = 128
BF16_SUBLANES = 16
VMEM_LIMIT = 56 * 1024 * 1024

TOKENS = BATCH * SEQ
LN_TILE = 1024
QKVU_CHUNK = 512
POOL_CHUNK = 256
POOL_HALO = 16
MIX_ROWS = 8
MIX_TILE = MIX_ROWS * GRID_W
GATE_BLOCK = 256
POOL_OUT_BLOCK = 512
FFN_TILE = 512
FFN_HALO = BF16_SUBLANES

f32 = jnp.float32
bf16 = jnp.bfloat16


def _layer_norm(xf, g, b):
    mu = jnp.mean(xf, axis=-1, keepdims=True)
    xc = xf - mu
    var = jnp.mean(xc * xc, axis=-1, keepdims=True)
    return xc * lax.rsqrt(var + LN_EPS) * g + b


def _dot(a, b):
    return jnp.dot(a, b, preferred_element_type=f32)


def _ln_kernel(x_ref, g_ref, b_ref, o_ref):
    o_ref[...] = _layer_norm(x_ref[...], g_ref[...], b_ref[...])


def _input_ln(x2d, g, b):
    return pl.pallas_call(
        _ln_kernel,
        grid=(TOKENS // LN_TILE,),
        in_specs=[
            pl.BlockSpec((LN_TILE, D_MODEL), lambda i: (i, 0)),
            pl.BlockSpec((1, D_MODEL), lambda i: (0, 0)),
            pl.BlockSpec((1, D_MODEL), lambda i: (0, 0)),
        ],
        out_specs=pl.BlockSpec((LN_TILE, D_MODEL), lambda i: (i, 0)),
        out_shape=jax.ShapeDtypeStruct((TOKENS, D_MODEL), f32),
        compiler_params=pltpu.CompilerParams(
            dimension_semantics=("arbitrary",), vmem_limit_bytes=VMEM_LIMIT),
        name="input_ln",
    )(x2d, g, b)


def _qkvu_kernel(h_ref, w_ref, b_ref, pw_ref, ps_ref,
                 qlo_ref, qhi_ref, k_ref, v_ref, pooled_ref,
                 upad_sc, pd_sc):
    lane = lax.broadcasted_iota(jnp.int32, (QKVU_CHUNK, D_ATTN), 1)
    even_head = (lane % LANES) < HEAD_DIM

    zeros_halo = jnp.zeros((POOL_HALO, D_POOL), f32)
    upad_sc[0:POOL_HALO, :] = zeros_halo
    upad_sc[POOL_HALO + SEQ:POOL_HALO + SEQ + POOL_HALO, :] = zeros_halo

    for c in range(SEQ // QKVU_CHUNK):
        rows = slice(c * QKVU_CHUNK, (c + 1) * QKVU_CHUNK)
        hb = h_ref[rows, :].astype(bf16)
        q = (_dot(hb, w_ref[:, 0:D_ATTN]) + b_ref[:, 0:D_ATTN]) * QK_SCALE
        qlo_ref[rows, :] = jnp.where(even_head, q, 0.0).astype(bf16)
        qhi_ref[rows, :] = jnp.where(even_head, 0.0, q).astype(bf16)
        k = _dot(hb, w_ref[:, D_ATTN:2 * D_ATTN]) + b_ref[:, D_ATTN:2 * D_ATTN]
        k_ref[rows, :] = k.astype(bf16)
        v = _dot(hb, w_ref[:, 2 * D_ATTN:3 * D_ATTN]) + b_ref[:, 2 * D_ATTN:3 * D_ATTN]
        v_ref[rows, :] = v.astype(bf16)
        u = _dot(hb, w_ref[:, 3 * D_ATTN:3 * D_ATTN + D_POOL]) + b_ref[:, 3 * D_ATTN:3 * D_ATTN + D_POOL]
        upad_sc[POOL_HALO + c * QKVU_CHUNK:POOL_HALO + (c + 1) * QKVU_CHUNK, :] = u

    for c in range(SEQ // POOL_CHUNK):
        t = lax.broadcasted_iota(jnp.int32, (POOL_CHUNK, 1), 0) + c * POOL_CHUNK
        base = POOL_HALO + c * POOL_CHUNK
        for g, w in enumerate(POOL_WINDOWS):
            half = w // 2
            cols = slice(g * POOL_GROUP_DIM, (g + 1) * POOL_GROUP_DIM)
            acc = upad_sc[base - half:base - half + POOL_CHUNK, cols]
            for j in range(-half + 1, half):
                acc = acc + upad_sc[base + j:base + j + POOL_CHUNK, cols]
            cnt = (jnp.minimum(t + half, SEQ) - jnp.maximum(t - half, 0)).astype(f32)
            centre = upad_sc[base:base + POOL_CHUNK, cols]
            pd_sc[c * POOL_CHUNK:(c + 1) * POOL_CHUNK, cols] = (acc / cnt - centre).astype(bf16)

    for g in range(len(POOL_WINDOWS)):
        cols = slice(g * POOL_GROUP_DIM, (g + 1) * POOL_GROUP_DIM)
        pooled = _dot(pd_sc[:, cols], pw_ref[g]) * ps_ref[:, cols]
        pooled_ref[:, cols] = pooled.astype(bf16)


def _qkvu(h3d, w_qkvu, b_qkvu, pool_w, pool_scale):
    seq_spec = lambda width: pl.BlockSpec((None, SEQ, width), lambda b: (b, 0, 0))
    full = lambda shape: pl.BlockSpec(shape, lambda b: (0,) * len(shape))
    out_sds = jax.ShapeDtypeStruct((BATCH, SEQ, D_ATTN), bf16)
    return pl.pallas_call(
        _qkvu_kernel,
        grid=(BATCH,),
        in_specs=[
            seq_spec(D_MODEL),
            full((D_MODEL, 3 * D_ATTN + D_POOL)),
            full((1, 3 * D_ATTN + D_POOL)),
            full((len(POOL_WINDOWS), POOL_GROUP_DIM, POOL_GROUP_DIM)),
            full((1, D_POOL)),
        ],
        out_specs=[seq_spec(D_ATTN)] * 5,
        out_shape=[out_sds] * 5,
        scratch_shapes=[
            pltpu.VMEM((SEQ + 2 * POOL_HALO, D_POOL), f32),
            pltpu.VMEM((SEQ, D_POOL), bf16),
        ],
        compiler_params=pltpu.CompilerParams(
            dimension_semantics=("arbitrary",), vmem_limit_bytes=VMEM_LIMIT),
        name="qkvu",
    )(h3d, w_qkvu, b_qkvu, pool_w, pool_scale)


def _mix_kernel(h_ref, qlo_ref, qhi_ref, k_ref, v_ref, pooled_ref, bias_ref, valid_ref,
                wg_ref, bg_ref, wa_ref, wp_ref, wm_ref, g_ref, b_ref,
                o_ref, attn_sc, s_sc, e_sc, l_sc, hb_sc, gsig_sc, ypool_sc, *, spread, group_rows):
    c = pl.program_id(1)
    even_head = lax.broadcasted_iota(jnp.int32, (GRID_W, LANES), 1) < HEAD_DIM

    def window(rr):
        r = c * MIX_ROWS + rr
        rs = jnp.clip(r - KH // 2, 0, ROWS - KH)
        return r - rs, pl.multiple_of(rs * GRID_W, GRID_W)

    def scores(rr, slot):
        _, k0 = window(rr)
        rows = slice(rr * GRID_W, (rr + 1) * GRID_W)
        for p in range(N_PAIRS):
            cols = slice(p * LANES, (p + 1) * LANES)
            lhs = jnp.concatenate([qlo_ref[rows, cols], qhi_ref[rows, cols]], axis=0)
            kw = k_ref[pl.ds(k0, WIN_KEYS), cols]
            s_sc[slot, p] = lax.dot_general(lhs, kw, (((1,), (1,)), ((), ())),
                                            preferred_element_type=f32)

    def softmax(rr, slot):
        delta, _ = window(rr)
        for p in range(N_PAIRS):
            for g in range(2 * GRID_W // group_rows):
                rows = slice(g * group_rows, (g + 1) * group_rows)
                valid = valid_ref[rows, :] > 0.5
                parts = []
                for j in range(WIN_KEYS // LANES):
                    sj = (s_sc[slot, p, rows, j * LANES:(j + 1) * LANES]
                          + bias_ref[2 * j + KH - 1 - delta, p, rows, :])
                    parts.append(jnp.where(valid, sj, NEG_INF))
                m = functools.reduce(jnp.maximum, parts)
                m = jnp.max(m, axis=-1, keepdims=True)
                es = [jnp.exp(sj - m) for sj in parts]
                l_sc[slot, p, rows, :] = jnp.sum(functools.reduce(lambda x, y: x + y, es),
                                                 axis=-1, keepdims=True)
                for j, ej in enumerate(es):
                    e_sc[slot, p, rows, j * LANES:(j + 1) * LANES] = ej.astype(bf16)

    def weighted_values(rr, slot):
        _, k0 = window(rr)
        rows = slice(rr * GRID_W, (rr + 1) * GRID_W)
        for p in range(N_PAIRS):
            cols = slice(p * LANES, (p + 1) * LANES)
            vw = v_ref[pl.ds(k0, WIN_KEYS), cols]
            o = _dot(e_sc[slot, p], vw) / l_sc[slot, p]
            o = jnp.where(even_head, o[:GRID_W], o[GRID_W:])
            attn_sc[rows, cols] = o.astype(bf16)

    if spread:
        hb_sc[...] = h_ref[...].astype(bf16)

    gate_block_cols = abs(spread) if spread else GATE_BLOCK

    def gate_block(j):
        cols = slice(j * gate_block_cols, (j + 1) * gate_block_cols)
        gsig_sc[:, cols] = jax.nn.sigmoid(_dot(hb_sc[...], wg_ref[:, cols]) + bg_ref[:, cols])

    def pool_block(j):
        cols = slice(j * POOL_OUT_BLOCK, (j + 1) * POOL_OUT_BLOCK)
        ypool_sc[:, cols] = _dot(pooled_ref[...], wp_ref[:, cols])

    n_gate_blocks = 2 * D_MODEL // gate_block_cols
    n_pool_blocks = D_MODEL // POOL_OUT_BLOCK
    if spread < 0:
        for j in range(n_gate_blocks):
            gate_block(j)
        for j in range(n_pool_blocks):
            pool_block(j)

    for it in range(MIX_ROWS + 2):
        if it < MIX_ROWS:
            scores(it, it % 2)
        if spread > 0 and it < n_gate_blocks:
            gate_block(it)
        elif spread > 0 and it < n_gate_blocks + n_pool_blocks:
            pool_block(it - n_gate_blocks)
        if 1 <= it <= MIX_ROWS:
            softmax(it - 1, (it - 1) % 2)
        if it >= 2:
            weighted_values(it - 2, it % 2)

    if spread:
        merged = (gsig_sc[:, 0:D_MODEL] * _dot(attn_sc[...], wa_ref[...])
                  + gsig_sc[:, D_MODEL:2 * D_MODEL] * ypool_sc[...])
        z = ALPHA * h_ref[...] + _dot(merged.astype(bf16), wm_ref[...])
    else:
        h = h_ref[...]
        hb = h.astype(bf16)
        ga = jax.nn.sigmoid(_dot(hb, wg_ref[:, 0:D_MODEL]) + bg_ref[:, 0:D_MODEL])
        merged = ga * _dot(attn_sc[...], wa_ref[...])
        gb = jax.nn.sigmoid(_dot(hb, wg_ref[:, D_MODEL:2 * D_MODEL]) + bg_ref[:, D_MODEL:2 * D_MODEL])
        merged = merged + gb * _dot(pooled_ref[...], wp_ref[...])
        z = ALPHA * h + _dot(merged.astype(bf16), wm_ref[...])
    o_ref[...] = _layer_norm(z, g_ref[...], b_ref[...])


def _mix(h3d, qlo, qhi, k, v, pooled, bias_tbl, valid_tbl,
         w_gate, b_gate, w_attn_out, w_pool_out, w_mix_out, ln_g, ln_b, *, spread, group_rows):
    tile = lambda width: pl.BlockSpec((None, MIX_TILE, width), lambda b, c: (b, c, 0))
    seq = lambda width: pl.BlockSpec((None, SEQ, width), lambda b, c: (b, 0, 0))
    full = lambda shape: pl.BlockSpec(shape, lambda b, c: (0,) * len(shape))
    return pl.pallas_call(
        functools.partial(_mix_kernel, spread=spread, group_rows=group_rows),
        grid=(BATCH, SEQ // MIX_TILE),
        in_specs=[
            tile(D_MODEL), tile(D_ATTN), tile(D_ATTN), seq(D_ATTN), seq(D_ATTN), tile(D_POOL),
            full((2 * KH - 2, N_PAIRS, 2 * GRID_W, LANES)),
            full((2 * GRID_W, LANES)),
            full((D_MODEL, 2 * D_MODEL)), full((1, 2 * D_MODEL)),
            full((D_ATTN, D_MODEL)), full((D_POOL, D_MODEL)), full((D_MODEL, D_MODEL)),
            full((1, D_MODEL)), full((1, D_MODEL)),
        ],
        out_specs=tile(D_MODEL),
        out_shape=jax.ShapeDtypeStruct((BATCH, SEQ, D_MODEL), f32),
        scratch_shapes=[
            pltpu.VMEM((MIX_TILE, D_ATTN), bf16),
            pltpu.VMEM((2, N_PAIRS, 2 * GRID_W, WIN_KEYS), f32),
            pltpu.VMEM((2, N_PAIRS, 2 * GRID_W, WIN_KEYS), bf16),
            pltpu.VMEM((2, N_PAIRS, 2 * GRID_W, 1), f32),
            pltpu.VMEM((MIX_TILE, D_MODEL), bf16),
            pltpu.VMEM((MIX_TILE, 2 * D_MODEL), f32),
            pltpu.VMEM((MIX_TILE, D_MODEL), f32),
        ],
        compiler_params=pltpu.CompilerParams(
            dimension_semantics=("arbitrary", "arbitrary"), vmem_limit_bytes=VMEM_LIMIT),
        name="mix",
    )(h3d, qlo, qhi, k, v, pooled, bias_tbl, valid_tbl,
      w_gate, b_gate, w_attn_out, w_pool_out, w_mix_out, ln_g, ln_b)


def _ffn_kernel(layer_ref, h_ref, hprev_ref, hnext_ref, p_ref,
                wv_ref, wgt_ref, cw_ref, cb_ref, wd_ref, wpg_ref, wpp_ref, g_ref, b_ref,
                o_ref, hext_sc, val_sc, gate_sc, act_sc, acc_sc, *, pair_down, roll_conv, act_groups):
    del layer_ref
    i = pl.program_id(0)
    tiles_per_seq = SEQ // FFN_TILE
    pos = i % tiles_per_seq
    keep_prev = (pos != 0).astype(f32)
    keep_next = (pos != tiles_per_seq - 1).astype(f32)

    h = h_ref[...]
    hb = h.astype(bf16)
    main = slice(FFN_HALO, FFN_HALO + FFN_TILE)
    hext_sc[main, :] = hb
    hext_sc[0:FFN_HALO, :] = (hprev_ref[...] * keep_prev).astype(bf16)
    hext_sc[FFN_HALO + FFN_TILE:, :] = (hnext_ref[...] * keep_next).astype(bf16)

    ple = jax.nn.sigmoid(_dot(hb, wpg_ref[...])) * _dot(p_ref[...].astype(bf16), wpp_ref[...])
    acc_sc[...] = ALPHA * h + ple

    def up(c, slot):
        cols = slice(c * FF_CHUNK, (c + 1) * FF_CHUNK)
        val_sc[slot] = _dot(hext_sc[main, :], wv_ref[:, cols])
        gate_sc[slot] = _dot(hext_sc[...], wgt_ref[:, cols])

    rows_per_group = FFN_TILE // act_groups

    def activate(c, slot):
        cols = slice(c * FF_CHUNK, (c + 1) * FF_CHUNK)
        aslot, acols = ((c // 2) % 2, slice((c % 2) * FF_CHUNK, (c % 2 + 1) * FF_CHUNK)) if pair_down \
            else (slot, slice(0, FF_CHUNK))
        for g in range(act_groups):
            r0 = FFN_HALO + g * rows_per_group
            if roll_conv:
                ext = gate_sc[slot, r0 - 8:r0 + rows_per_group + 8, :]
                n = rows_per_group + 16
                prev = pltpu.roll(ext, 1, 0)[8:8 + rows_per_group]
                nxt = pltpu.roll(ext, n - 1, 0)[8:8 + rows_per_group]
                cur = ext[8:8 + rows_per_group]
            else:
                prev = gate_sc[slot, r0 - 1:r0 - 1 + rows_per_group, :]
                cur = gate_sc[slot, r0:r0 + rows_per_group, :]
                nxt = gate_sc[slot, r0 + 1:r0 + 1 + rows_per_group, :]
            y = prev * cw_ref[0:1, cols] + cur * cw_ref[1:2, cols] + nxt * cw_ref[2:3, cols] + cb_ref[:, cols]
            act = 0.5 * y * (1.0 + lax.erf(y * np.float32(np.sqrt(0.5))))
            rows = slice(g * rows_per_group, (g + 1) * rows_per_group)
            act_sc[aslot, rows, acols] = (act * val_sc[slot, rows, :]).astype(bf16)

    def down(c0, n_chunks, aslot):
        k = n_chunks * FF_CHUNK
        acc_sc[...] += _dot(act_sc[aslot, :, 0:k], wd_ref[c0 * FF_CHUNK:c0 * FF_CHUNK + k, :])

    for it in range(N_FF_CHUNKS + 2):
        if it < N_FF_CHUNKS:
            up(it, it % 2)
        if 1 <= it <= N_FF_CHUNKS:
            activate(it - 1, (it - 1) % 2)
        c = it - 2
        if c >= 0 and not pair_down:
            down(c, 1, c % 2)
        elif c >= 0 and c % 2 == 1:
            down(c - 1, 2, (c // 2) % 2)
        elif c == N_FF_CHUNKS - 1:
            down(c, 1, (c // 2) % 2)
    o_ref[...] = _layer_norm(acc_sc[...], g_ref[...], b_ref[...])


def _ffn(layer, h2d, p3d, w_val, w_gate, conv_w, conv_b, w_down, w_ple_gate, w_ple_proj, ln_g, ln_b,
         **variant):
    halo_blocks = FFN_TILE // FFN_HALO
    n_halo_blocks = TOKENS // FFN_HALO
    full = lambda shape: pl.BlockSpec(shape, lambda i, l: (0,) * len(shape))
    grid_spec = pltpu.PrefetchScalarGridSpec(
        num_scalar_prefetch=1,
        grid=(TOKENS // FFN_TILE,),
        in_specs=[
            pl.BlockSpec((FFN_TILE, D_MODEL), lambda i, l: (i, 0)),
            pl.BlockSpec((FFN_HALO, D_MODEL), lambda i, l: (jnp.maximum(i * halo_blocks - 1, 0), 0)),
            pl.BlockSpec((FFN_HALO, D_MODEL),
                         lambda i, l: (jnp.minimum((i + 1) * halo_blocks, n_halo_blocks - 1), 0)),
            pl.BlockSpec((None, FFN_TILE, PLE_DIM), lambda i, l: (l[0], i, 0)),
            full((D_MODEL, D_FF)),
            full((D_MODEL, D_FF)),
            full((3, D_FF)),
            full((1, D_FF)),
            full((D_FF, D_MODEL)),
            full((D_MODEL, D_MODEL)),
            full((PLE_DIM, D_MODEL)),
            full((1, D_MODEL)), full((1, D_MODEL)),
        ],
        out_specs=pl.BlockSpec((FFN_TILE, D_MODEL), lambda i, l: (i, 0)),
        scratch_shapes=[
            pltpu.VMEM((FFN_TILE + 2 * FFN_HALO, D_MODEL), bf16),
            pltpu.VMEM((2, FFN_TILE, FF_CHUNK), f32),
            pltpu.VMEM((2, FFN_TILE + 2 * FFN_HALO, FF_CHUNK), f32),
            pltpu.VMEM((2, FFN_TILE, 2 * FF_CHUNK), bf16),
            pltpu.VMEM((FFN_TILE, D_MODEL), f32),
        ],
    )
    return pl.pallas_call(
        functools.partial(_ffn_kernel, **variant),
        grid_spec=grid_spec,
        out_shape=jax.ShapeDtypeStruct((TOKENS, D_MODEL), f32),
        compiler_params=pltpu.CompilerParams(
            dimension_semantics=("arbitrary",), vmem_limit_bytes=VMEM_LIMIT),
        name="ffn",
    )(layer, h2d, h2d, h2d, p3d, w_val, w_gate, conv_w, conv_b, w_down, w_ple_gate, w_ple_proj, ln_g, ln_b)


def _column_geometry():
    qc = np.arange(GRID_W)[:, None]
    kc = np.arange(GRID_W)[None, :]
    col_start = np.clip(qc - KW // 2, 0, GRID_W - KW)
    valid = (kc >= col_start) & (kc < col_start + KW)
    col_off = np.clip(kc - qc, -(KW - 1), KW - 1) + KW - 1
    onehot = (col_off[..., None] == np.arange(2 * KW - 1)).astype(np.float32)
    valid_tbl = np.tile(valid.astype(np.float32), (2, 2))
    return onehot, valid_tbl


def _bias_tables(rpb):
    onehot, _ = _column_geometry()
    select = np.einsum('ij,qkc->qikjc', np.eye(2, dtype=np.float32), onehot)
    select = select.reshape(GRID_W, LANES, 2, 2 * KW - 1)
    two_rows = jnp.stack([rpb[:, :, :-1], rpb[:, :, 1:]], axis=3)
    tbl = jnp.einsum('lhrjc,qmjc->lrhqm', two_rows, jnp.asarray(select),
                     precision=lax.Precision.HIGHEST)
    return tbl.reshape(DEPTH, 2 * KH - 2, N_PAIRS, 2 * GRID_W, LANES)


MIX_VARIANTS = (
    dict(spread=0, group_rows=128),
    dict(spread=256, group_rows=128),
    dict(spread=-1024, group_rows=128),
    dict(spread=512, group_rows=128),
)
FFN_VARIANTS = (
    dict(pair_down=False, roll_conv=False, act_groups=1),
    dict(pair_down=True, roll_conv=False, act_groups=1),
    dict(pair_down=False, roll_conv=True, act_groups=1),
    dict(pair_down=True, roll_conv=True, act_groups=1),
)


def kernel(x, p, ln_in_g, ln_in_b, w_in, b_in, rpb, w_attn_out, pool_w, pool_scale, w_pool_out,
           w_mix_out, ln1_g, ln1_b, w_up, conv_w, conv_b, w_down, w_ple_gate, w_ple_proj,
           ln2_g, ln2_b):
    n_qkvu = 3 * D_ATTN + D_POOL
    valid_tbl = jnp.asarray(_column_geometry()[1])
    bias_tbl = _bias_tables(rpb)
    p3d = p.reshape(DEPTH, TOKENS, PLE_DIM)
    row = lambda a: a.reshape(1, -1)

    h = _input_ln(x.reshape(TOKENS, D_MODEL), row(ln_in_g), row(ln_in_b))
    for i in range(DEPTH):
        h3d = h.reshape(BATCH, SEQ, D_MODEL)
        qlo, qhi, k, v, pooled = _qkvu(
            h3d, w_in[i, :, :n_qkvu].astype(bf16), row(b_in[i, :n_qkvu]),
            pool_w[i].astype(bf16), row(pool_scale[i]))
        h3d = _mix(h3d, qlo, qhi, k, v, pooled, bias_tbl[i], valid_tbl,
                   w_in[i, :, n_qkvu:].astype(bf16), row(b_in[i, n_qkvu:]),
                   w_attn_out[i].astype(bf16), w_pool_out[i].astype(bf16), w_mix_out[i].astype(bf16),
                   row(ln1_g[i]), row(ln1_b[i]), **MIX_VARIANTS[i])
        h = _ffn(jnp.full((1,), i, jnp.int32), h3d.reshape(TOKENS, D_MODEL), p3d,
                 w_up[i, :, :D_FF].astype(bf16), w_up[i, :, D_FF:].astype(bf16),
                 conv_w[i], row(conv_b[i]), w_down[i].astype(bf16),
                 w_ple_gate[i].astype(bf16), w_ple_proj[i].astype(bf16),
                 row(ln2_g[i]), row(ln2_b[i]), **FFN_VARIANTS[i])
    return h.reshape(BATCH, SEQ, D_MODEL)
```

```python
import functools

import numpy as np
import jax
import jax.numpy as jnp
from jax import lax
from jax.experimental import pallas as pl
from jax.experimental.pallas import tpu as pltpu

D_MODEL = 1024
BATCH = 8
SEQ = 2048
DEPTH = 4
GRID_W = 64
ROWS = SEQ // GRID_W
N_HEADS = 8
HEAD_DIM = 64
D_ATTN = N_HEADS * HEAD_DIM
N_PAIRS = N_HEADS // 2
KH = 8
KW = 16
WIN_KEYS = KH * GRID_W
POOL_WINDOWS = (2, 4, 8, 16)
D_POOL = 512
POOL_GROUP_DIM = 128
D_FF = 2816
FF_CHUNK = 256
N_FF_CHUNKS = D_FF // FF_CHUNK
PLE_DIM = 256
ALPHA = (2 * DEPTH) ** 0.25
LN_EPS = 1e-5
NEG_INF = -1e30
QK_SCALE = HEAD_DIM ** -0.5

LANES = 128
BF16_SUBLANES = 16
VMEM_LIMIT = 56 * 1024 * 1024

TOKENS = BATCH * SEQ
LN_TILE = 1024
QKVU_CHUNK = 512
POOL_CHUNK = 256
POOL_HALO = 16
MIX_ROWS = 8
MIX_TILE = MIX_ROWS * GRID_W
FFN_HALO = BF16_SUBLANES

f32 = jnp.float32
bf16 = jnp.bfloat16


def _layer_norm(xf, g, b):
    mu = jnp.mean(xf, axis=-1, keepdims=True)
    xc = xf - mu
    var = jnp.mean(xc * xc, axis=-1, keepdims=True)
    return xc * lax.rsqrt(var + LN_EPS) * g + b


def _dot(a, b):
    return jnp.dot(a, b, preferred_element_type=f32)


def _ln_kernel(x_ref, g_ref, b_ref, o_ref):
    o_ref[...] = _layer_norm(x_ref[...], g_ref[...], b_ref[...])


def _input_ln(x2d, g, b):
    return pl.pallas_call(
        _ln_kernel,
        grid=(TOKENS // LN_TILE,),
        in_specs=[
            pl.BlockSpec((LN_TILE, D_MODEL), lambda i: (i, 0)),
            pl.BlockSpec((1, D_MODEL), lambda i: (0, 0)),
            pl.BlockSpec((1, D_MODEL), lambda i: (0, 0)),
        ],
        out_specs=pl.BlockSpec((LN_TILE, D_MODEL), lambda i: (i, 0)),
        out_shape=jax.ShapeDtypeStruct((TOKENS, D_MODEL), f32),
        compiler_params=pltpu.CompilerParams(
            dimension_semantics=("arbitrary",), vmem_limit_bytes=VMEM_LIMIT),
        name="input_ln",
    )(x2d, g, b)


def _qkvu_kernel(h_ref, w_ref, b_ref, pw_ref, ps_ref,
                 qlo_ref, qhi_ref, k_ref, v_ref, pooled_ref,
                 upad_sc, pd_sc):
    lane = lax.broadcasted_iota(jnp.int32, (QKVU_CHUNK, D_ATTN), 1)
    even_head = (lane % LANES) < HEAD_DIM

    zeros_halo = jnp.zeros((POOL_HALO, D_POOL), f32)
    upad_sc[0:POOL_HALO, :] = zeros_halo
    upad_sc[POOL_HALO + SEQ:POOL_HALO + SEQ + POOL_HALO, :] = zeros_halo

    for c in range(SEQ // QKVU_CHUNK):
        rows = slice(c * QKVU_CHUNK, (c + 1) * QKVU_CHUNK)
        hb = h_ref[rows, :].astype(bf16)
        q = (_dot(hb, w_ref[:, 0:D_ATTN]) + b_ref[:, 0:D_ATTN]) * QK_SCALE
        qlo_ref[rows, :] = jnp.where(even_head, q, 0.0).astype(bf16)
        qhi_ref[rows, :] = jnp.where(even_head, 0.0, q).astype(bf16)
        k = _dot(hb, w_ref[:, D_ATTN:2 * D_ATTN]) + b_ref[:, D_ATTN:2 * D_ATTN]
        k_ref[rows, :] = k.astype(bf16)
        v = _dot(hb, w_ref[:, 2 * D_ATTN:3 * D_ATTN]) + b_ref[:, 2 * D_ATTN:3 * D_ATTN]
        v_ref[rows, :] = v.astype(bf16)
        u = _dot(hb, w_ref[:, 3 * D_ATTN:3 * D_ATTN + D_POOL]) + b_ref[:, 3 * D_ATTN:3 * D_ATTN + D_POOL]
        upad_sc[POOL_HALO + c * QKVU_CHUNK:POOL_HALO + (c + 1) * QKVU_CHUNK, :] = u

    for c in range(SEQ // POOL_CHUNK):
        t = lax.broadcasted_iota(jnp.int32, (POOL_CHUNK, 1), 0) + c * POOL_CHUNK
        base = POOL_HALO + c * POOL_CHUNK
        for g, w in enumerate(POOL_WINDOWS):
            half = w // 2
            cols = slice(g * POOL_GROUP_DIM, (g + 1) * POOL_GROUP_DIM)
            acc = upad_sc[base - half:base - half + POOL_CHUNK, cols]
            for j in range(-half + 1, half):
                acc = acc + upad_sc[base + j:base + j + POOL_CHUNK, cols]
            cnt = (jnp.minimum(t + half, SEQ) - jnp.maximum(t - half, 0)).astype(f32)
            centre = upad_sc[base:base + POOL_CHUNK, cols]
            pd_sc[c * POOL_CHUNK:(c + 1) * POOL_CHUNK, cols] = (acc / cnt - centre).astype(bf16)

    for g in range(len(POOL_WINDOWS)):
        cols = slice(g * POOL_GROUP_DIM, (g + 1) * POOL_GROUP_DIM)
        pooled = _dot(pd_sc[:, cols], pw_ref[g]) * ps_ref[:, cols]
        pooled_ref[:, cols] = pooled.astype(bf16)


def _qkvu(h3d, w_qkvu, b_qkvu, pool_w, pool_scale):
    seq_spec = lambda width: pl.BlockSpec((None, SEQ, width), lambda b: (b, 0, 0))
    full = lambda shape: pl.BlockSpec(shape, lambda b: (0,) * len(shape))
    out_sds = jax.ShapeDtypeStruct((BATCH, SEQ, D_ATTN), bf16)
    return pl.pallas_call(
        _qkvu_kernel,
        grid=(BATCH,),
        in_specs=[
            seq_spec(D_MODEL),
            full((D_MODEL, 3 * D_ATTN + D_POOL)),
            full((1, 3 * D_ATTN + D_POOL)),
            full((len(POOL_WINDOWS), POOL_GROUP_DIM, POOL_GROUP_DIM)),
            full((1, D_POOL)),
        ],
        out_specs=[seq_spec(D_ATTN)] * 5,
        out_shape=[out_sds] * 5,
        scratch_shapes=[
            pltpu.VMEM((SEQ + 2 * POOL_HALO, D_POOL), f32),
            pltpu.VMEM((SEQ, D_POOL), bf16),
        ],
        compiler_params=pltpu.CompilerParams(
            dimension_semantics=("arbitrary",), vmem_limit_bytes=VMEM_LIMIT),
        name="qkvu",
    )(h3d, w_qkvu, b_qkvu, pool_w, pool_scale)


def _mix_kernel(h_ref, qlo_ref, qhi_ref, k_ref, v_ref, pooled_ref, bias_ref, valid_ref,
                wg_ref, bg_ref, wa_ref, wp_ref, wm_ref, g_ref, b_ref,
                o_ref, attn_sc, s_sc, e_sc, l_sc, *, fuse_softmax, dense_parts):
    c = pl.program_id(1)
    even_head = lax.broadcasted_iota(jnp.int32, (GRID_W, LANES), 1) < HEAD_DIM

    def window(rr):
        r = c * MIX_ROWS + rr
        rs = jnp.clip(r - KH // 2, 0, ROWS - KH)
        return r - rs, pl.multiple_of(rs * GRID_W, GRID_W)

    def score_tile(rr, p):
        _, k0 = window(rr)
        rows = slice(rr * GRID_W, (rr + 1) * GRID_W)
        cols = slice(p * LANES, (p + 1) * LANES)
        lhs = jnp.concatenate([qlo_ref[rows, cols], qhi_ref[rows, cols]], axis=0)
        kw = k_ref[pl.ds(k0, WIN_KEYS), cols]
        return lax.dot_general(lhs, kw, (((1,), (1,)), ((), ())),
                               preferred_element_type=f32)

    def softmax_tile(rr, slot, p, lane_block):
        delta, _ = window(rr)
        valid = valid_ref[...] > 0.5
        parts = []
        for j in range(WIN_KEYS // LANES):
            sj = lane_block(j) + bias_ref[2 * j + KH - 1 - delta, p]
            parts.append(jnp.where(valid, sj, NEG_INF))
        m = functools.reduce(jnp.maximum, parts)
        m = jnp.max(m, axis=-1, keepdims=True)
        es = [jnp.exp(sj - m) for sj in parts]
        l_sc[slot, p] = jnp.sum(functools.reduce(lambda x, y: x + y, es), axis=-1, keepdims=True)
        for j, ej in enumerate(es):
            e_sc[slot, p, :, j * LANES:(j + 1) * LANES] = ej.astype(bf16)

    def scores(rr, slot):
        for p in range(N_PAIRS):
            if fuse_softmax:
                s = score_tile(rr, p)
                softmax_tile(rr, slot, p, lambda j: s[:, j * LANES:(j + 1) * LANES])
            else:
                s_sc[slot, p] = score_tile(rr, p)

    def softmax(rr, slot):
        for p in range(N_PAIRS):
            softmax_tile(rr, slot, p, lambda j: s_sc[slot, p, :, j * LANES:(j + 1) * LANES])

    def weighted_values(rr, slot):
        _, k0 = window(rr)
        rows = slice(rr * GRID_W, (rr + 1) * GRID_W)
        for p in range(N_PAIRS):
            cols = slice(p * LANES, (p + 1) * LANES)
            vw = v_ref[pl.ds(k0, WIN_KEYS), cols]
            o = _dot(e_sc[slot, p], vw) / l_sc[slot, p]
            o = jnp.where(even_head, o[:GRID_W], o[GRID_W:])
            attn_sc[rows, cols] = o.astype(bf16)

    if fuse_softmax:
        for it in range(MIX_ROWS + 1):
            if it < MIX_ROWS:
                scores(it, it % 2)
            if it >= 1:
                weighted_values(it - 1, (it - 1) % 2)
    else:
        for it in range(MIX_ROWS + 2):
            if it < MIX_ROWS:
                scores(it, it % 2)
            if 1 <= it <= MIX_ROWS:
                softmax(it - 1, (it - 1) % 2)
            if it >= 2:
                weighted_values(it - 2, it % 2)

    part_rows = MIX_TILE // dense_parts
    for part in range(dense_parts):
        rows = slice(part * part_rows, (part + 1) * part_rows)
        h = h_ref[rows, :]
        hb = h.astype(bf16)
        ga = jax.nn.sigmoid(_dot(hb, wg_ref[:, 0:D_MODEL]) + bg_ref[:, 0:D_MODEL])
        merged = ga * _dot(attn_sc[rows, :], wa_ref[...])
        gb = jax.nn.sigmoid(_dot(hb, wg_ref[:, D_MODEL:2 * D_MODEL]) + bg_ref[:, D_MODEL:2 * D_MODEL])
        merged = merged + gb * _dot(pooled_ref[rows, :], wp_ref[...])
        z = ALPHA * h + _dot(merged.astype(bf16), wm_ref[...])
        o_ref[rows, :] = _layer_norm(z, g_ref[...], b_ref[...])


def _mix(h3d, qlo, qhi, k, v, pooled, bias_tbl, valid_tbl,
         w_gate, b_gate, w_attn_out, w_pool_out, w_mix_out, ln_g, ln_b, **variant):
    tile = lambda width: pl.BlockSpec((None, MIX_TILE, width), lambda b, c: (b, c, 0))
    seq = lambda width: pl.BlockSpec((None, SEQ, width), lambda b, c: (b, 0, 0))
    full = lambda shape: pl.BlockSpec(shape, lambda b, c: (0,) * len(shape))
    return pl.pallas_call(
        functools.partial(_mix_kernel, **variant),
        grid=(BATCH, SEQ // MIX_TILE),
        in_specs=[
            tile(D_MODEL), tile(D_ATTN), tile(D_ATTN), seq(D_ATTN), seq(D_ATTN), tile(D_POOL),
            full((2 * KH - 2, N_PAIRS, 2 * GRID_W, LANES)),
            full((2 * GRID_W, LANES)),
            full((D_MODEL, 2 * D_MODEL)), full((1, 2 * D_MODEL)),
            full((D_ATTN, D_MODEL)), full((D_POOL, D_MODEL)), full((D_MODEL, D_MODEL)),
            full((1, D_MODEL)), full((1, D_MODEL)),
        ],
        out_specs=tile(D_MODEL),
        out_shape=jax.ShapeDtypeStruct((BATCH, SEQ, D_MODEL), f32),
        scratch_shapes=[
            pltpu.VMEM((MIX_TILE, D_ATTN), bf16),
            pltpu.VMEM((2, N_PAIRS, 2 * GRID_W, WIN_KEYS), f32),
            pltpu.VMEM((2, N_PAIRS, 2 * GRID_W, WIN_KEYS), bf16),
            pltpu.VMEM((2, N_PAIRS, 2 * GRID_W, 1), f32),
        ],
        compiler_params=pltpu.CompilerParams(
            dimension_semantics=("arbitrary", "arbitrary"), vmem_limit_bytes=VMEM_LIMIT),
        name="mix",
    )(h3d, qlo, qhi, k, v, pooled, bias_tbl, valid_tbl,
      w_gate, b_gate, w_attn_out, w_pool_out, w_mix_out, ln_g, ln_b)


def _ffn_kernel(layer_ref, h_ref, hprev_ref, hnext_ref, p_ref,
                wv_ref, wgt_ref, cw_ref, cb_ref, wd_ref, wpg_ref, wpp_ref, g_ref, b_ref,
                o_ref, hext_sc, val_sc, gate_sc, act_sc, *, tile):
    del layer_ref
    i = pl.program_id(0)
    tiles_per_seq = SEQ // tile
    pos = i % tiles_per_seq
    keep_prev = (pos != 0).astype(f32)
    keep_next = (pos != tiles_per_seq - 1).astype(f32)

    h = h_ref[...]
    hb = h.astype(bf16)
    main = slice(FFN_HALO, FFN_HALO + tile)
    hext_sc[main, :] = hb
    hext_sc[0:FFN_HALO, :] = (hprev_ref[...] * keep_prev).astype(bf16)
    hext_sc[FFN_HALO + tile:, :] = (hnext_ref[...] * keep_next).astype(bf16)

    ple = jax.nn.sigmoid(_dot(hb, wpg_ref[...])) * _dot(p_ref[...].astype(bf16), wpp_ref[...])
    o_ref[...] = ALPHA * h + ple

    def up(c, slot):
        cols = slice(c * FF_CHUNK, (c + 1) * FF_CHUNK)
        val_sc[slot] = _dot(hext_sc[main, :], wv_ref[:, cols])
        gate_sc[slot] = _dot(hext_sc[...], wgt_ref[:, cols])

    def activate(c, slot):
        cols = slice(c * FF_CHUNK, (c + 1) * FF_CHUNK)
        gate = gate_sc[slot]
        n = tile + 2 * FFN_HALO
        prev = pltpu.roll(gate, 1, 0)[main]
        nxt = pltpu.roll(gate, n - 1, 0)[main]
        y = prev * cw_ref[0:1, cols] + gate[main] * cw_ref[1:2, cols] + nxt * cw_ref[2:3, cols] + cb_ref[:, cols]
        act = 0.5 * y * (1.0 + lax.erf(y * np.float32(np.sqrt(0.5))))
        half = slice((c % 2) * FF_CHUNK, (c % 2 + 1) * FF_CHUNK)
        act_sc[(c // 2) % 2, :, half] = (act * val_sc[slot]).astype(bf16)

    def down(c0, n_chunks):
        k = n_chunks * FF_CHUNK
        o_ref[...] += _dot(act_sc[(c0 // 2) % 2, :, 0:k], wd_ref[c0 * FF_CHUNK:c0 * FF_CHUNK + k, :])

    for it in range(N_FF_CHUNKS + 2):
        if it < N_FF_CHUNKS:
            up(it, it % 2)
        if 1 <= it <= N_FF_CHUNKS:
            activate(it - 1, (it - 1) % 2)
        c = it - 2
        if c >= 0 and c % 2 == 1:
            down(c - 1, 2)
        elif c == N_FF_CHUNKS - 1:
            down(c, 1)
    o_ref[...] = _layer_norm(o_ref[...], g_ref[...], b_ref[...])


def _ffn(layer, h2d, p3d, w_val, w_gate, conv_w, conv_b, w_down, w_ple_gate, w_ple_proj, ln_g, ln_b, *, tile):
    halo_blocks = tile // FFN_HALO
    n_halo_blocks = TOKENS // FFN_HALO
    full = lambda shape: pl.BlockSpec(shape, lambda i, l: (0,) * len(shape))
    grid_spec = pltpu.PrefetchScalarGridSpec(
        num_scalar_prefetch=1,
        grid=(TOKENS // tile,),
        in_specs=[
            pl.BlockSpec((tile, D_MODEL), lambda i, l: (i, 0)),
            pl.BlockSpec((FFN_HALO, D_MODEL), lambda i, l: (jnp.maximum(i * halo_blocks - 1, 0), 0)),
            pl.BlockSpec((FFN_HALO, D_MODEL),
                         lambda i, l: (jnp.minimum((i + 1) * halo_blocks, n_halo_blocks - 1), 0)),
            pl.BlockSpec((None, tile, PLE_DIM), lambda i, l: (l[0], i, 0)),
            full((D_MODEL, D_FF)),
            full((D_MODEL, D_FF)),
            full((3, D_FF)),
            full((1, D_FF)),
            full((D_FF, D_MODEL)),
            full((D_MODEL, D_MODEL)),
            full((PLE_DIM, D_MODEL)),
            full((1, D_MODEL)), full((1, D_MODEL)),
        ],
        out_specs=pl.BlockSpec((tile, D_MODEL), lambda i, l: (i, 0)),
        scratch_shapes=[
            pltpu.VMEM((tile + 2 * FFN_HALO, D_MODEL), bf16),
            pltpu.VMEM((2, tile, FF_CHUNK), f32),
            pltpu.VMEM((2, tile + 2 * FFN_HALO, FF_CHUNK), f32),
            pltpu.VMEM((2, tile, 2 * FF_CHUNK), bf16),
        ],
    )
    return pl.pallas_call(
        functools.partial(_ffn_kernel, tile=tile),
        grid_spec=grid_spec,
        out_shape=jax.ShapeDtypeStruct((TOKENS, D_MODEL), f32),
        compiler_params=pltpu.CompilerParams(
            dimension_semantics=("arbitrary",), vmem_limit_bytes=VMEM_LIMIT),
        name="ffn",
    )(layer, h2d, h2d, h2d, p3d, w_val, w_gate, conv_w, conv_b, w_down, w_ple_gate, w_ple_proj, ln_g, ln_b)


def _column_geometry():
    qc = np.arange(GRID_W)[:, None]
    kc = np.arange(GRID_W)[None, :]
    col_start = np.clip(qc - KW // 2, 0, GRID_W - KW)
    valid = (kc >= col_start) & (kc < col_start + KW)
    col_off = np.clip(kc - qc, -(KW - 1), KW - 1) + KW - 1
    onehot = (col_off[..., None] == np.arange(2 * KW - 1)).astype(np.float32)
    valid_tbl = np.tile(valid.astype(np.float32), (2, 2))
    return onehot, valid_tbl


def _bias_tables(rpb):
    onehot, _ = _column_geometry()
    select = np.einsum('ij,qkc->qikjc', np.eye(2, dtype=np.float32), onehot)
    select = select.reshape(GRID_W, LANES, 2, 2 * KW - 1)
    two_rows = jnp.stack([rpb[:, :, :-1], rpb[:, :, 1:]], axis=3)
    tbl = jnp.einsum('lhrjc,qmjc->lrhqm', two_rows, jnp.asarray(select),
                     precision=lax.Precision.HIGHEST)
    return tbl.reshape(DEPTH, 2 * KH - 2, N_PAIRS, 2 * GRID_W, LANES)


MIX_VARIANTS = (
    dict(fuse_softmax=False, dense_parts=1),
    dict(fuse_softmax=True, dense_parts=1),
    dict(fuse_softmax=False, dense_parts=2),
    dict(fuse_softmax=True, dense_parts=2),
)
FFN_VARIANTS = (dict(tile=512), dict(tile=512), dict(tile=1024), dict(tile=1024))


def kernel(x, p, ln_in_g, ln_in_b, w_in, b_in, rpb, w_attn_out, pool_w, pool_scale, w_pool_out,
           w_mix_out, ln1_g, ln1_b, w_up, conv_w, conv_b, w_down, w_ple_gate, w_ple_proj,
           ln2_g, ln2_b):
    n_qkvu = 3 * D_ATTN + D_POOL
    valid_tbl = jnp.asarray(_column_geometry()[1])
    bias_tbl = _bias_tables(rpb)
    p3d = p.reshape(DEPTH, TOKENS, PLE_DIM)
    row = lambda a: a.reshape(1, -1)

    h = _input_ln(x.reshape(TOKENS, D_MODEL), row(ln_in_g), row(ln_in_b))
    for i in range(DEPTH):
        h3d = h.reshape(BATCH, SEQ, D_MODEL)
        qlo, qhi, k, v, pooled = _qkvu(
            h3d, w_in[i, :, :n_qkvu].astype(bf16), row(b_in[i, :n_qkvu]),
            pool_w[i].astype(bf16), row(pool_scale[i]))
        h3d = _mix(h3d, qlo, qhi, k, v, pooled, bias_tbl[i], valid_tbl,
                   w_in[i, :, n_qkvu:].astype(bf16), row(b_in[i, n_qkvu:]),
                   w_attn_out[i].astype(bf16), w_pool_out[i].astype(bf16), w_mix_out[i].astype(bf16),
                   row(ln1_g[i]), row(ln1_b[i]), **MIX_VARIANTS[i])
        h = _ffn(jnp.full((1,), i, jnp.int32), h3d.reshape(TOKENS, D_MODEL), p3d,
                 w_up[i, :, :D_FF].astype(bf16), w_up[i, :, D_FF:].astype(bf16),
                 conv_w[i], row(conv_b[i]), w_down[i].astype(bf16),
                 w_ple_gate[i].astype(bf16), w_ple_proj[i].astype(bf16),
                 row(ln2_g[i]), row(ln2_b[i]), **FFN_VARIANTS[i])
    return h.reshape(BATCH, SEQ, D_MODEL)
```

```python
import functools

import numpy as np
import jax
import jax.numpy as jnp
from jax import lax
from jax.experimental import pallas as pl
from jax.experimental.pallas import tpu as pltpu

D_MODEL = 1024
BATCH = 8
SEQ = 2048
DEPTH = 4
GRID_W = 64
ROWS = SEQ // GRID_W
N_HEADS = 8
HEAD_DIM = 64
D_ATTN = N_HEADS * HEAD_DIM
N_PAIRS = N_HEADS // 2
KH = 8
KW = 16
WIN_KEYS = KH * GRID_W
POOL_WINDOWS = (2, 4, 8, 16)
D_POOL = 512
POOL_GROUP_DIM = 128
D_FF = 2816
FF_CHUNK = 256
N_FF_CHUNKS = D_FF // FF_CHUNK
PLE_DIM = 256
ALPHA = (2 * DEPTH) ** 0.25
LN_EPS = 1e-5
NEG_INF = -1e30
QK_SCALE = HEAD_DIM ** -0.5

LANES = 128
BF16_SUBLANES = 16
VMEM_LIMIT = 56 * 1024 * 1024

TOKENS = BATCH * SEQ
LN_TILE = 1024
QKVU_CHUNK = 512
POOL_CHUNK = 256
POOL_HALO = 16
MIX_ROWS = 8
MIX_TILE = MIX_ROWS * GRID_W
MIX_DENSE_PARTS = 2
FFN_TILE = 512
FFN_HALO = BF16_SUBLANES

f32 = jnp.float32
bf16 = jnp.bfloat16


def _layer_norm(xf, g, b):
    mu = jnp.mean(xf, axis=-1, keepdims=True)
    xc = xf - mu
    var = jnp.mean(xc * xc, axis=-1, keepdims=True)
    return xc * lax.rsqrt(var + LN_EPS) * g + b


def _dot(a, b):
    return jnp.dot(a, b, preferred_element_type=f32)


def _layer_block(shape, col_block=0, single_buffer=False):
    def index_map(*args):
        layer = args[-1][0]
        return (layer,) + (0,) * (len(shape) - 1) + (col_block,)
    mode = dict(pipeline_mode=pl.Buffered(1)) if single_buffer else {}
    return pl.BlockSpec((None,) + tuple(shape), index_map, **mode)


def _ln_kernel(x_ref, g_ref, b_ref, o_ref):
    o_ref[...] = _layer_norm(x_ref[...], g_ref[...], b_ref[...])


def _input_ln(x2d, g, b):
    return pl.pallas_call(
        _ln_kernel,
        grid=(TOKENS // LN_TILE,),
        in_specs=[
            pl.BlockSpec((LN_TILE, D_MODEL), lambda i: (i, 0)),
            pl.BlockSpec((1, D_MODEL), lambda i: (0, 0)),
            pl.BlockSpec((1, D_MODEL), lambda i: (0, 0)),
        ],
        out_specs=pl.BlockSpec((LN_TILE, D_MODEL), lambda i: (i, 0)),
        out_shape=jax.ShapeDtypeStruct((TOKENS, D_MODEL), f32),
        compiler_params=pltpu.CompilerParams(
            dimension_semantics=("arbitrary",), vmem_limit_bytes=VMEM_LIMIT),
        name="input_ln",
    )(x2d, g, b)


def _qkvu_kernel(layer_ref, h_ref, w_ref, b_ref, pw_ref, ps_ref,
                 qlo_ref, qhi_ref, k_ref, v_ref, pooled_ref,
                 upad_sc, pd_sc):
    del layer_ref
    lane = lax.broadcasted_iota(jnp.int32, (QKVU_CHUNK, D_ATTN), 1)
    even_head = (lane % LANES) < HEAD_DIM

    zeros_halo = jnp.zeros((POOL_HALO, D_POOL), f32)
    upad_sc[0:POOL_HALO, :] = zeros_halo
    upad_sc[POOL_HALO + SEQ:POOL_HALO + SEQ + POOL_HALO, :] = zeros_halo

    for c in range(SEQ // QKVU_CHUNK):
        rows = slice(c * QKVU_CHUNK, (c + 1) * QKVU_CHUNK)
        hb = h_ref[rows, :].astype(bf16)
        q = (_dot(hb, w_ref[:, 0:D_ATTN]) + b_ref[:, 0:D_ATTN]) * QK_SCALE
        qlo_ref[rows, :] = jnp.where(even_head, q, 0.0).astype(bf16)
        qhi_ref[rows, :] = jnp.where(even_head, 0.0, q).astype(bf16)
        k = _dot(hb, w_ref[:, D_ATTN:2 * D_ATTN]) + b_ref[:, D_ATTN:2 * D_ATTN]
        k_ref[rows, :] = k.astype(bf16)
        v = _dot(hb, w_ref[:, 2 * D_ATTN:3 * D_ATTN]) + b_ref[:, 2 * D_ATTN:3 * D_ATTN]
        v_ref[rows, :] = v.astype(bf16)
        u = _dot(hb, w_ref[:, 3 * D_ATTN:3 * D_ATTN + D_POOL]) + b_ref[:, 3 * D_ATTN:3 * D_ATTN + D_POOL]
        upad_sc[POOL_HALO + c * QKVU_CHUNK:POOL_HALO + (c + 1) * QKVU_CHUNK, :] = u

    for c in range(SEQ // POOL_CHUNK):
        t = lax.broadcasted_iota(jnp.int32, (POOL_CHUNK, 1), 0) + c * POOL_CHUNK
        base = POOL_HALO + c * POOL_CHUNK
        for g, w in enumerate(POOL_WINDOWS):
            half = w // 2
            cols = slice(g * POOL_GROUP_DIM, (g + 1) * POOL_GROUP_DIM)
            acc = upad_sc[base - half:base - half + POOL_CHUNK, cols]
            for j in range(-half + 1, half):
                acc = acc + upad_sc[base + j:base + j + POOL_CHUNK, cols]
            cnt = (jnp.minimum(t + half, SEQ) - jnp.maximum(t - half, 0)).astype(f32)
            centre = upad_sc[base:base + POOL_CHUNK, cols]
            pd_sc[c * POOL_CHUNK:(c + 1) * POOL_CHUNK, cols] = (acc / cnt - centre).astype(bf16)

    for g in range(len(POOL_WINDOWS)):
        cols = slice(g * POOL_GROUP_DIM, (g + 1) * POOL_GROUP_DIM)
        pooled = _dot(pd_sc[:, cols], pw_ref[g]) * ps_ref[:, cols]
        pooled_ref[:, cols] = pooled.astype(bf16)


def _qkvu(layer, h3d, w_in, b_in, pool_w, pool_scale):
    n_qkvu = 3 * D_ATTN + D_POOL
    seq_spec = lambda width: pl.BlockSpec((None, SEQ, width), lambda b, l: (b, 0, 0))
    out_sds = jax.ShapeDtypeStruct((BATCH, SEQ, D_ATTN), bf16)
    grid_spec = pltpu.PrefetchScalarGridSpec(
        num_scalar_prefetch=1,
        grid=(BATCH,),
        in_specs=[
            seq_spec(D_MODEL),
            _layer_block((D_MODEL, n_qkvu), single_buffer=True),
            _layer_block((1, n_qkvu)),
            _layer_block((len(POOL_WINDOWS), POOL_GROUP_DIM, POOL_GROUP_DIM)),
            _layer_block((1, D_POOL)),
        ],
        out_specs=[seq_spec(D_ATTN)] * 5,
        scratch_shapes=[
            pltpu.VMEM((SEQ + 2 * POOL_HALO, D_POOL), f32),
            pltpu.VMEM((SEQ, D_POOL), bf16),
        ],
    )
    return pl.pallas_call(
        _qkvu_kernel,
        grid_spec=grid_spec,
        out_shape=[out_sds] * 5,
        compiler_params=pltpu.CompilerParams(
            dimension_semantics=("arbitrary",), vmem_limit_bytes=VMEM_LIMIT),
        name="qkvu",
    )(layer, h3d, w_in, b_in, pool_w, pool_scale)


def _mix_kernel(layer_ref, h_ref, qlo_ref, qhi_ref, k_ref, v_ref, pooled_ref, bias_ref, valid_ref,
                wg_ref, bg_ref, wa_ref, wp_ref, wm_ref, g_ref, b_ref,
                o_ref, attn_sc, e_sc, l_sc):
    del layer_ref
    c = pl.program_id(1)
    even_head = lax.broadcasted_iota(jnp.int32, (GRID_W, LANES), 1) < HEAD_DIM

    def window(rr):
        r = c * MIX_ROWS + rr
        rs = jnp.clip(r - KH // 2, 0, ROWS - KH)
        return r - rs, pl.multiple_of(rs * GRID_W, GRID_W)

    def scores_softmax(rr, slot):
        delta, k0 = window(rr)
        rows = slice(rr * GRID_W, (rr + 1) * GRID_W)
        valid = valid_ref[...] > 0.5
        for p in range(N_PAIRS):
            cols = slice(p * LANES, (p + 1) * LANES)
            lhs = jnp.concatenate([qlo_ref[rows, cols], qhi_ref[rows, cols]], axis=0)
            kw = k_ref[pl.ds(k0, WIN_KEYS), cols]
            s = lax.dot_general(lhs, kw, (((1,), (1,)), ((), ())),
                                preferred_element_type=f32)
            parts = []
            for j in range(WIN_KEYS // LANES):
                sj = s[:, j * LANES:(j + 1) * LANES] + bias_ref[2 * j + KH - 1 - delta, p]
                parts.append(jnp.where(valid, sj, NEG_INF))
            m = functools.reduce(jnp.maximum, parts)
            m = jnp.max(m, axis=-1, keepdims=True)
            es = [jnp.exp(sj - m) for sj in parts]
            l_sc[slot, p] = jnp.sum(functools.reduce(lambda x, y: x + y, es), axis=-1, keepdims=True)
            for j, ej in enumerate(es):
                e_sc[slot, p, :, j * LANES:(j + 1) * LANES] = ej.astype(bf16)

    def weighted_values(rr, slot):
        _, k0 = window(rr)
        rows = slice(rr * GRID_W, (rr + 1) * GRID_W)
        for p in range(N_PAIRS):
            cols = slice(p * LANES, (p + 1) * LANES)
            vw = v_ref[pl.ds(k0, WIN_KEYS), cols]
            o = _dot(e_sc[slot, p], vw) / l_sc[slot, p]
            o = jnp.where(even_head, o[:GRID_W], o[GRID_W:])
            attn_sc[rows, cols] = o.astype(bf16)

    for it in range(MIX_ROWS + 1):
        if it < MIX_ROWS:
            scores_softmax(it, it % 2)
        if it >= 1:
            weighted_values(it - 1, (it - 1) % 2)

    part_rows = MIX_TILE // MIX_DENSE_PARTS
    for part in range(MIX_DENSE_PARTS):
        rows = slice(part * part_rows, (part + 1) * part_rows)
        h = h_ref[rows, :]
        hb = h.astype(bf16)
        ga = jax.nn.sigmoid(_dot(hb, wg_ref[:, 0:D_MODEL]) + bg_ref[:, 0:D_MODEL])
        merged = ga * _dot(attn_sc[rows, :], wa_ref[...])
        gb = jax.nn.sigmoid(_dot(hb, wg_ref[:, D_MODEL:2 * D_MODEL]) + bg_ref[:, D_MODEL:2 * D_MODEL])
        merged = merged + gb * _dot(pooled_ref[rows, :], wp_ref[...])
        z = ALPHA * h + _dot(merged.astype(bf16), wm_ref[...])
        o_ref[rows, :] = _layer_norm(z, g_ref[...], b_ref[...])


def _mix(layer, h3d, qlo, qhi, k, v, pooled, bias_tbl, valid_tbl,
         w_in, b_in, w_attn_out, w_pool_out, w_mix_out, ln_g, ln_b):
    tile = lambda width: pl.BlockSpec((None, MIX_TILE, width), lambda b, c, l: (b, c, 0))
    seq = lambda width: pl.BlockSpec((None, SEQ, width), lambda b, c, l: (b, 0, 0))
    grid_spec = pltpu.PrefetchScalarGridSpec(
        num_scalar_prefetch=1,
        grid=(BATCH, SEQ // MIX_TILE),
        in_specs=[
            tile(D_MODEL), tile(D_ATTN), tile(D_ATTN), seq(D_ATTN), seq(D_ATTN), tile(D_POOL),
            _layer_block((2 * KH - 2, N_PAIRS, 2 * GRID_W, LANES), single_buffer=True),
            pl.BlockSpec((2 * GRID_W, LANES), lambda b, c, l: (0, 0)),
            _layer_block((D_MODEL, 2 * D_MODEL), col_block=1, single_buffer=True),
            _layer_block((1, 2 * D_MODEL), col_block=1),
            _layer_block((D_ATTN, D_MODEL), single_buffer=True),
            _layer_block((D_POOL, D_MODEL), single_buffer=True),
            _layer_block((D_MODEL, D_MODEL), single_buffer=True),
            _layer_block((1, D_MODEL)), _layer_block((1, D_MODEL)),
        ],
        out_specs=tile(D_MODEL),
        scratch_shapes=[
            pltpu.VMEM((MIX_TILE, D_ATTN), bf16),
            pltpu.VMEM((2, N_PAIRS, 2 * GRID_W, WIN_KEYS), bf16),
            pltpu.VMEM((2, N_PAIRS, 2 * GRID_W, 1), f32),
        ],
    )
    return pl.pallas_call(
        _mix_kernel,
        grid_spec=grid_spec,
        out_shape=jax.ShapeDtypeStruct((BATCH, SEQ, D_MODEL), f32),
        compiler_params=pltpu.CompilerParams(
            dimension_semantics=("arbitrary", "arbitrary"), vmem_limit_bytes=VMEM_LIMIT),
        name="mix",
    )(layer, h3d, qlo, qhi, k, v, pooled, bias_tbl, valid_tbl,
      w_in, b_in, w_attn_out, w_pool_out, w_mix_out, ln_g, ln_b)


def _ffn_kernel(layer_ref, h_ref, hprev_ref, hnext_ref, p_ref,
                wv_ref, wgt_ref, cw_ref, cb_ref, wd_ref, wpg_ref, wpp_ref, g_ref, b_ref,
                o_ref, hext_sc, val_sc, gate_sc, act_sc, acc_sc):
    del layer_ref
    i = pl.program_id(0)
    tiles_per_seq = SEQ // FFN_TILE
    pos = i % tiles_per_seq
    keep_prev = (pos != 0).astype(f32)
    keep_next = (pos != tiles_per_seq - 1).astype(f32)

    h = h_ref[...]
    hb = h.astype(bf16)
    main = slice(FFN_HALO, FFN_HALO + FFN_TILE)
    hext_sc[main, :] = hb
    hext_sc[0:FFN_HALO, :] = (hprev_ref[...] * keep_prev).astype(bf16)
    hext_sc[FFN_HALO + FFN_TILE:, :] = (hnext_ref[...] * keep_next).astype(bf16)

    ple = jax.nn.sigmoid(_dot(hb, wpg_ref[...])) * _dot(p_ref[...].astype(bf16), wpp_ref[...])
    acc_sc[...] = ALPHA * h + ple

    def up(c, slot):
        cols = slice(c * FF_CHUNK, (c + 1) * FF_CHUNK)
        val_sc[slot] = _dot(hext_sc[main, :], wv_ref[:, cols])
        gate_sc[slot] = _dot(hext_sc[...], wgt_ref[:, cols])

    def activate(c, slot):
        cols = slice(c * FF_CHUNK, (c + 1) * FF_CHUNK)
        gate = gate_sc[slot]
        n = FFN_TILE + 2 * FFN_HALO
        prev = pltpu.roll(gate, 1, 0)[main]
        nxt = pltpu.roll(gate, n - 1, 0)[main]
        y = prev * cw_ref[0:1, cols] + gate[main] * cw_ref[1:2, cols] + nxt * cw_ref[2:3, cols] + cb_ref[:, cols]
        act = 0.5 * y * (1.0 + lax.erf(y * np.float32(np.sqrt(0.5))))
        half = slice((c % 2) * FF_CHUNK, (c % 2 + 1) * FF_CHUNK)
        act_sc[(c // 2) % 2, :, half] = (act * val_sc[slot]).astype(bf16)

    def down(c0, n_chunks):
        k = n_chunks * FF_CHUNK
        acc_sc[...] += _dot(act_sc[(c0 // 2) % 2, :, 0:k], wd_ref[c0 * FF_CHUNK:c0 * FF_CHUNK + k, :])

    for it in range(N_FF_CHUNKS + 2):
        if it < N_FF_CHUNKS:
            up(it, it % 2)
        if 1 <= it <= N_FF_CHUNKS:
            activate(it - 1, (it - 1) % 2)
        c = it - 2
        if c >= 0 and c % 2 == 1:
            down(c - 1, 2)
        elif c == N_FF_CHUNKS - 1:
            down(c, 1)
    o_ref[...] = _layer_norm(acc_sc[...], g_ref[...], b_ref[...])


def _ffn(layer, h2d, p3d, w_up, conv_w, conv_b, w_down, w_ple_gate, w_ple_proj, ln_g, ln_b):
    halo_blocks = FFN_TILE // FFN_HALO
    n_halo_blocks = TOKENS // FFN_HALO
    grid_spec = pltpu.PrefetchScalarGridSpec(
        num_scalar_prefetch=1,
        grid=(TOKENS // FFN_TILE,),
        in_specs=[
            pl.BlockSpec((FFN_TILE, D_MODEL), lambda i, l: (i, 0)),
            pl.BlockSpec((FFN_HALO, D_MODEL), lambda i, l: (jnp.maximum(i * halo_blocks - 1, 0), 0)),
            pl.BlockSpec((FFN_HALO, D_MODEL),
                         lambda i, l: (jnp.minimum((i + 1) * halo_blocks, n_halo_blocks - 1), 0)),
            pl.BlockSpec((None, FFN_TILE, PLE_DIM), lambda i, l: (l[0], i, 0)),
            _layer_block((D_MODEL, D_FF), col_block=0, single_buffer=True),
            _layer_block((D_MODEL, D_FF), col_block=1, single_buffer=True),
            _layer_block((3, D_FF)),
            _layer_block((1, D_FF)),
            _layer_block((D_FF, D_MODEL), single_buffer=True),
            _layer_block((D_MODEL, D_MODEL), single_buffer=True),
            _layer_block((PLE_DIM, D_MODEL), single_buffer=True),
            _layer_block((1, D_MODEL)), _layer_block((1, D_MODEL)),
        ],
        out_specs=pl.BlockSpec((FFN_TILE, D_MODEL), lambda i, l: (i, 0)),
        scratch_shapes=[
            pltpu.VMEM((FFN_TILE + 2 * FFN_HALO, D_MODEL), bf16),
            pltpu.VMEM((2, FFN_TILE, FF_CHUNK), f32),
            pltpu.VMEM((2, FFN_TILE + 2 * FFN_HALO, FF_CHUNK), f32),
            pltpu.VMEM((2, FFN_TILE, 2 * FF_CHUNK), bf16),
            pltpu.VMEM((FFN_TILE, D_MODEL), f32),
        ],
    )
    return pl.pallas_call(
        _ffn_kernel,
        grid_spec=grid_spec,
        out_shape=jax.ShapeDtypeStruct((TOKENS, D_MODEL), f32),
        compiler_params=pltpu.CompilerParams(
            dimension_semantics=("arbitrary",), vmem_limit_bytes=VMEM_LIMIT),
        name="ffn",
    )(layer, h2d, h2d, h2d, p3d, w_up, w_up, conv_w, conv_b, w_down, w_ple_gate, w_ple_proj, ln_g, ln_b)


def _column_geometry():
    qc = np.arange(GRID_W)[:, None]
    kc = np.arange(GRID_W)[None, :]
    col_start = np.clip(qc - KW // 2, 0, GRID_W - KW)
    valid = (kc >= col_start) & (kc < col_start + KW)
    col_off = np.clip(kc - qc, -(KW - 1), KW - 1) + KW - 1
    onehot = (col_off[..., None] == np.arange(2 * KW - 1)).astype(np.float32)
    valid_tbl = np.tile(valid.astype(np.float32), (2, 2))
    return onehot, valid_tbl


def _bias_tables(rpb):
    onehot, _ = _column_geometry()
    select = np.einsum('ij,qkc->qikjc', np.eye(2, dtype=np.float32), onehot)
    select = select.reshape(GRID_W, LANES, 2, 2 * KW - 1)
    two_rows = jnp.stack([rpb[:, :, :-1], rpb[:, :, 1:]], axis=3)
    tbl = jnp.einsum('lhrjc,qmjc->lrhqm', two_rows, jnp.asarray(select),
                     precision=lax.Precision.HIGHEST)
    return tbl.reshape(DEPTH, 2 * KH - 2, N_PAIRS, 2 * GRID_W, LANES)


def kernel(x, p, ln_in_g, ln_in_b, w_in, b_in, rpb, w_attn_out, pool_w, pool_scale, w_pool_out,
           w_mix_out, ln1_g, ln1_b, w_up, conv_w, conv_b, w_down, w_ple_gate, w_ple_proj,
           ln2_g, ln2_b):
    valid_tbl = jnp.asarray(_column_geometry()[1])
    bias_tbl = _bias_tables(rpb)
    p3d = p.reshape(DEPTH, TOKENS, PLE_DIM)
    row = lambda a: a.reshape(1, -1)
    rows = lambda a: a.reshape(DEPTH, 1, -1)
    w_in_b, w_up_b, w_down_b = w_in.astype(bf16), w_up.astype(bf16), w_down.astype(bf16)
    w_attn_out_b, w_pool_out_b, w_mix_out_b = (w.astype(bf16) for w in (w_attn_out, w_pool_out, w_mix_out))
    pool_w_b, w_ple_gate_b, w_ple_proj_b = (w.astype(bf16) for w in (pool_w, w_ple_gate, w_ple_proj))
    b_in_r, pool_scale_r, conv_b_r = rows(b_in), rows(pool_scale), rows(conv_b)
    ln1_g_r, ln1_b_r, ln2_g_r, ln2_b_r = rows(ln1_g), rows(ln1_b), rows(ln2_g), rows(ln2_b)

    h = _input_ln(x.reshape(TOKENS, D_MODEL), row(ln_in_g), row(ln_in_b))
    for i in range(DEPTH):
        layer = jnp.full((1,), i, jnp.int32)
        h3d = h.reshape(BATCH, SEQ, D_MODEL)
        qlo, qhi, k, v, pooled = _qkvu(layer, h3d, w_in_b, b_in_r, pool_w_b, pool_scale_r)
        h3d = _mix(layer, h3d, qlo, qhi, k, v, pooled, bias_tbl, valid_tbl,
                   w_in_b, b_in_r, w_attn_out_b, w_pool_out_b, w_mix_out_b, ln1_g_r, ln1_b_r)
        h = _ffn(layer, h3d.reshape(TOKENS, D_MODEL), p3d, w_up_b, conv_w, conv_b_r, w_down_b,
                 w_ple_gate_b, w_ple_proj_b, ln2_g_r, ln2_b_r)
    return h.reshape(BATCH, SEQ, D_MODEL)
```

```python
import functools

import numpy as np
import jax
import jax.numpy as jnp
from jax import lax
from jax.experimental import pallas as pl
from jax.experimental.pallas import tpu as pltpu

D_MODEL = 1024
BATCH = 8
SEQ = 2048
DEPTH = 4
GRID_W = 64
ROWS = SEQ // GRID_W
N_HEADS = 8
HEAD_DIM = 64
D_ATTN = N_HEADS * HEAD_DIM
N_PAIRS = N_HEADS // 2
KH = 8
KW = 16
WIN_KEYS = KH * GRID_W
POOL_WINDOWS = (2, 4, 8, 16)
D_POOL = 512
POOL_GROUP_DIM = 128
D_FF = 2816
FF_CHUNK = 256
N_FF_CHUNKS = D_FF // FF_CHUNK
PLE_DIM = 256
ALPHA = (2 * DEPTH) ** 0.25
LN_EPS = 1e-5
NEG_INF = -1e30
QK_SCALE = HEAD_DIM ** -0.5

LANES = 128
BF16_SUBLANES = 16
VMEM_LIMIT = 56 * 1024 * 1024

TOKENS = BATCH * SEQ
LN_TILE = 1024
QKVU_CHUNK = 512
POOL_HALO = 16
MIX_ROWS = 8
MIX_TILE = MIX_ROWS * GRID_W
MIX_DENSE_PARTS = 2
FFN_TILE = 512
FFN_HALO = BF16_SUBLANES
STAGE_ROWS = 256

f32 = jnp.float32
bf16 = jnp.bfloat16


def _layer_norm(xf, g, b):
    mu = jnp.mean(xf, axis=-1, keepdims=True)
    xc = xf - mu
    var = jnp.mean(xc * xc, axis=-1, keepdims=True)
    return xc * lax.rsqrt(var + LN_EPS) * g + b


def _dot(a, b):
    return jnp.dot(a, b, preferred_element_type=f32)


def _layer_block(shape, col_block=0, single_buffer=False):
    def index_map(*args):
        layer = args[-1][0]
        return (layer,) + (0,) * (len(shape) - 1) + (col_block,)
    mode = dict(pipeline_mode=pl.Buffered(1)) if single_buffer else {}
    return pl.BlockSpec((None,) + tuple(shape), index_map, **mode)


def _load_weights_as_bf16(jobs, stage_sc, sem):
    def copy(j, slot):
        src, _, _, rows, cols = jobs[j]
        return pltpu.make_async_copy(src, stage_sc.at[slot, pl.ds(0, rows), pl.ds(0, cols)], sem.at[slot])

    copy(0, 0).start()
    for j, (_, dst, row0, rows, cols) in enumerate(jobs):
        slot = j % 2
        if j + 1 < len(jobs):
            copy(j + 1, 1 - slot).start()
        copy(j, slot).wait()
        dst[row0:row0 + rows, :] = stage_sc[slot, 0:rows, 0:cols].astype(bf16)


def _weight_jobs(w_hbm, layer, dst, n_rows, col0, cols):
    return [(w_hbm.at[layer, pl.ds(r0, min(STAGE_ROWS, n_rows - r0)), pl.ds(col0, cols)],
             dst, r0, min(STAGE_ROWS, n_rows - r0), cols)
            for r0 in range(0, n_rows, STAGE_ROWS)]


_HBM = pl.BlockSpec(memory_space=pl.ANY)


def _ln_kernel(x_ref, g_ref, b_ref, o_ref):
    o_ref[...] = _layer_norm(x_ref[...], g_ref[...], b_ref[...])


def _input_ln(x2d, g, b):
    return pl.pallas_call(
        _ln_kernel,
        grid=(TOKENS // LN_TILE,),
        in_specs=[
            pl.BlockSpec((LN_TILE, D_MODEL), lambda i: (i, 0)),
            pl.BlockSpec((1, D_MODEL), lambda i: (0, 0)),
            pl.BlockSpec((1, D_MODEL), lambda i: (0, 0)),
        ],
        out_specs=pl.BlockSpec((LN_TILE, D_MODEL), lambda i: (i, 0)),
        out_shape=jax.ShapeDtypeStruct((TOKENS, D_MODEL), f32),
        compiler_params=pltpu.CompilerParams(
            dimension_semantics=("arbitrary",), vmem_limit_bytes=VMEM_LIMIT),
        name="input_ln",
    )(x2d, g, b)


def _qkvu_kernel(layer_ref, h_ref, w_hbm, b_ref, pw_ref, ps_ref,
                 qlo_ref, qhi_ref, k_ref, v_ref, pooled_ref,
                 upad_sc, pd_sc, w_ref, stage_sc, sem):
    n_qkvu = 3 * D_ATTN + D_POOL

    @pl.when(pl.program_id(0) == 0)
    def _():
        _load_weights_as_bf16(_weight_jobs(w_hbm, layer_ref[0], w_ref, D_MODEL, 0, n_qkvu), stage_sc, sem)

    lane = lax.broadcasted_iota(jnp.int32, (QKVU_CHUNK, D_ATTN), 1)
    even_head = (lane % LANES) < HEAD_DIM

    zeros_halo = jnp.zeros((POOL_HALO, D_POOL), f32)
    upad_sc[0:POOL_HALO, :] = zeros_halo
    upad_sc[POOL_HALO + SEQ:POOL_HALO + SEQ + POOL_HALO, :] = zeros_halo

    for c in range(SEQ // QKVU_CHUNK):
        rows = slice(c * QKVU_CHUNK, (c + 1) * QKVU_CHUNK)
        hb = h_ref[rows, :].astype(bf16)
        q = (_dot(hb, w_ref[:, 0:D_ATTN]) + b_ref[:, 0:D_ATTN]) * QK_SCALE
        qlo_ref[rows, :] = jnp.where(even_head, q, 0.0).astype(bf16)
        qhi_ref[rows, :] = jnp.where(even_head, 0.0, q).astype(bf16)
        k = _dot(hb, w_ref[:, D_ATTN:2 * D_ATTN]) + b_ref[:, D_ATTN:2 * D_ATTN]
        k_ref[rows, :] = k.astype(bf16)
        v = _dot(hb, w_ref[:, 2 * D_ATTN:3 * D_ATTN]) + b_ref[:, 2 * D_ATTN:3 * D_ATTN]
        v_ref[rows, :] = v.astype(bf16)
        u = _dot(hb, w_ref[:, 3 * D_ATTN:3 * D_ATTN + D_POOL]) + b_ref[:, 3 * D_ATTN:3 * D_ATTN + D_POOL]
        upad_sc[POOL_HALO + c * QKVU_CHUNK:POOL_HALO + (c + 1) * QKVU_CHUNK, :] = u

    n_pad = SEQ + 2 * POOL_HALO
    seq_rows = slice(POOL_HALO, POOL_HALO + SEQ)
    t = lax.broadcasted_iota(jnp.int32, (SEQ, 1), 0)
    for g, w in enumerate(POOL_WINDOWS):
        half = w // 2
        cols = slice(g * POOL_GROUP_DIM, (g + 1) * POOL_GROUP_DIM)
        u = upad_sc[:, cols]
        win = pltpu.roll(u, 1, 0) + u
        reach = 1
        for _ in range(g):
            win = pltpu.roll(win, reach, 0) + pltpu.roll(win, n_pad - reach, 0)
            reach *= 2
        cnt = (jnp.minimum(t + half, SEQ) - jnp.maximum(t - half, 0)).astype(f32)
        pd_sc[:, cols] = (win[seq_rows] / cnt - u[seq_rows]).astype(bf16)

    for g in range(len(POOL_WINDOWS)):
        cols = slice(g * POOL_GROUP_DIM, (g + 1) * POOL_GROUP_DIM)
        pooled = _dot(pd_sc[:, cols], pw_ref[g]) * ps_ref[:, cols]
        pooled_ref[:, cols] = pooled.astype(bf16)


def _qkvu(layer, h3d, w_in, b_in, pool_w, pool_scale):
    n_qkvu = 3 * D_ATTN + D_POOL
    seq_spec = lambda width: pl.BlockSpec((None, SEQ, width), lambda b, l: (b, 0, 0))
    out_sds = jax.ShapeDtypeStruct((BATCH, SEQ, D_ATTN), bf16)
    grid_spec = pltpu.PrefetchScalarGridSpec(
        num_scalar_prefetch=1,
        grid=(BATCH,),
        in_specs=[
            seq_spec(D_MODEL),
            _HBM,
            _layer_block((1, n_qkvu)),
            _layer_block((len(POOL_WINDOWS), POOL_GROUP_DIM, POOL_GROUP_DIM)),
            _layer_block((1, D_POOL)),
        ],
        out_specs=[seq_spec(D_ATTN)] * 5,
        scratch_shapes=[
            pltpu.VMEM((SEQ + 2 * POOL_HALO, D_POOL), f32),
            pltpu.VMEM((SEQ, D_POOL), bf16),
            pltpu.VMEM((D_MODEL, n_qkvu), bf16),
            pltpu.VMEM((2, STAGE_ROWS, n_qkvu), f32),
            pltpu.SemaphoreType.DMA((2,)),
        ],
    )
    return pl.pallas_call(
        _qkvu_kernel,
        grid_spec=grid_spec,
        out_shape=[out_sds] * 5,
        compiler_params=pltpu.CompilerParams(
            dimension_semantics=("arbitrary",), vmem_limit_bytes=VMEM_LIMIT),
        name="qkvu",
    )(layer, h3d, w_in, b_in, pool_w, pool_scale)


def _mix_kernel(layer_ref, h_ref, qlo_ref, qhi_ref, k_ref, v_ref, pooled_ref, bias_ref, valid_ref,
                w_in_hbm, bg_ref, wa_hbm, wp_hbm, wm_hbm, g_ref, b_ref,
                o_ref, attn_sc, e_sc, l_sc, wg_ref, wa_ref, wp_ref, wm_ref, stage_sc, sem):
    c = pl.program_id(1)

    @pl.when((pl.program_id(0) == 0) & (c == 0))
    def _():
        layer = layer_ref[0]
        jobs = (_weight_jobs(w_in_hbm, layer, wg_ref, D_MODEL, 3 * D_ATTN + D_POOL, 2 * D_MODEL)
                + _weight_jobs(wa_hbm, layer, wa_ref, D_ATTN, 0, D_MODEL)
                + _weight_jobs(wp_hbm, layer, wp_ref, D_POOL, 0, D_MODEL)
                + _weight_jobs(wm_hbm, layer, wm_ref, D_MODEL, 0, D_MODEL))
        _load_weights_as_bf16(jobs, stage_sc, sem)

    even_head = lax.broadcasted_iota(jnp.int32, (GRID_W, LANES), 1) < HEAD_DIM

    def window(rr):
        r = c * MIX_ROWS + rr
        rs = jnp.clip(r - KH // 2, 0, ROWS - KH)
        return r - rs, pl.multiple_of(rs * GRID_W, GRID_W)

    def scores_softmax(rr, slot):
        delta, k0 = window(rr)
        rows = slice(rr * GRID_W, (rr + 1) * GRID_W)
        valid = valid_ref[...] > 0.5
        for p in range(N_PAIRS):
            cols = slice(p * LANES, (p + 1) * LANES)
            lhs = jnp.concatenate([qlo_ref[rows, cols], qhi_ref[rows, cols]], axis=0)
            kw = k_ref[pl.ds(k0, WIN_KEYS), cols]
            s = lax.dot_general(lhs, kw, (((1,), (1,)), ((), ())),
                                preferred_element_type=f32)
            parts = []
            for j in range(WIN_KEYS // LANES):
                sj = s[:, j * LANES:(j + 1) * LANES] + bias_ref[2 * j + KH - 1 - delta, p]
                parts.append(jnp.where(valid, sj, NEG_INF))
            m = functools.reduce(jnp.maximum, parts)
            m = jnp.max(m, axis=-1, keepdims=True)
            es = [jnp.exp(sj - m) for sj in parts]
            l_sc[slot, p] = jnp.sum(functools.reduce(lambda x, y: x + y, es), axis=-1, keepdims=True)
            for j, ej in enumerate(es):
                e_sc[slot, p, :, j * LANES:(j + 1) * LANES] = ej.astype(bf16)

    def weighted_values(rr, slot):
        _, k0 = window(rr)
        rows = slice(rr * GRID_W, (rr + 1) * GRID_W)
        for p in range(N_PAIRS):
            cols = slice(p * LANES, (p + 1) * LANES)
            vw = v_ref[pl.ds(k0, WIN_KEYS), cols]
            o = _dot(e_sc[slot, p], vw) / l_sc[slot, p]
            o = jnp.where(even_head, o[:GRID_W], o[GRID_W:])
            attn_sc[rows, cols] = o.astype(bf16)

    for it in range(MIX_ROWS + 1):
        if it < MIX_ROWS:
            scores_softmax(it, it % 2)
        if it >= 1:
            weighted_values(it - 1, (it - 1) % 2)

    part_rows = MIX_TILE // MIX_DENSE_PARTS
    for part in range(MIX_DENSE_PARTS):
        rows = slice(part * part_rows, (part + 1) * part_rows)
        h = h_ref[rows, :]
        hb = h.astype(bf16)
        ga = jax.nn.sigmoid(_dot(hb, wg_ref[:, 0:D_MODEL]) + bg_ref[:, 0:D_MODEL])
        merged = ga * _dot(attn_sc[rows, :], wa_ref[...])
        gb = jax.nn.sigmoid(_dot(hb, wg_ref[:, D_MODEL:2 * D_MODEL]) + bg_ref[:, D_MODEL:2 * D_MODEL])
        merged = merged + gb * _dot(pooled_ref[rows, :], wp_ref[...])
        z = ALPHA * h + _dot(merged.astype(bf16), wm_ref[...])
        o_ref[rows, :] = _layer_norm(z, g_ref[...], b_ref[...])


def _mix(layer, h3d, qlo, qhi, k, v, pooled, bias_tbl, valid_tbl,
         w_in, b_in, w_attn_out, w_pool_out, w_mix_out, ln_g, ln_b):
    tile = lambda width: pl.BlockSpec((None, MIX_TILE, width), lambda b, c, l: (b, c, 0))
    seq = lambda width: pl.BlockSpec((None, SEQ, width), lambda b, c, l: (b, 0, 0))
    grid_spec = pltpu.PrefetchScalarGridSpec(
        num_scalar_prefetch=1,
        grid=(BATCH, SEQ // MIX_TILE),
        in_specs=[
            tile(D_MODEL), tile(D_ATTN), tile(D_ATTN), seq(D_ATTN), seq(D_ATTN), tile(D_POOL),
            _layer_block((2 * KH - 2, N_PAIRS, 2 * GRID_W, LANES), single_buffer=True),
            pl.BlockSpec((2 * GRID_W, LANES), lambda b, c, l: (0, 0)),
            _HBM,
            _layer_block((1, 2 * D_MODEL), col_block=1),
            _HBM, _HBM, _HBM,
            _layer_block((1, D_MODEL)), _layer_block((1, D_MODEL)),
        ],
        out_specs=tile(D_MODEL),
        scratch_shapes=[
            pltpu.VMEM((MIX_TILE, D_ATTN), bf16),
            pltpu.VMEM((2, N_PAIRS, 2 * GRID_W, WIN_KEYS), bf16),
            pltpu.VMEM((2, N_PAIRS, 2 * GRID_W, 1), f32),
            pltpu.VMEM((D_MODEL, 2 * D_MODEL), bf16),
            pltpu.VMEM((D_ATTN, D_MODEL), bf16),
            pltpu.VMEM((D_POOL, D_MODEL), bf16),
            pltpu.VMEM((D_MODEL, D_MODEL), bf16),
            pltpu.VMEM((2, STAGE_ROWS, 2 * D_MODEL), f32),
            pltpu.SemaphoreType.DMA((2,)),
        ],
    )
    return pl.pallas_call(
        _mix_kernel,
        grid_spec=grid_spec,
        out_shape=jax.ShapeDtypeStruct((BATCH, SEQ, D_MODEL), f32),
        compiler_params=pltpu.CompilerParams(
            dimension_semantics=("arbitrary", "arbitrary"), vmem_limit_bytes=VMEM_LIMIT),
        name="mix",
    )(layer, h3d, qlo, qhi, k, v, pooled, bias_tbl, valid_tbl,
      w_in, b_in, w_attn_out, w_pool_out, w_mix_out, ln_g, ln_b)


def _ffn_kernel(layer_ref, h_ref, hprev_ref, hnext_ref, p_ref,
                w_up_hbm, cw_ref, cb_ref, wd_hbm, wpg_hbm, wpp_hbm, g_ref, b_ref,
                o_ref, hext_sc, val_sc, gate_sc, act_sc, acc_sc,
                wv_ref, wgt_ref, wd_ref, wpg_ref, wpp_ref, stage_sc, sem):
    i = pl.program_id(0)

    @pl.when(i == 0)
    def _():
        layer = layer_ref[0]
        jobs = (_weight_jobs(wpg_hbm, layer, wpg_ref, D_MODEL, 0, D_MODEL)
                + _weight_jobs(wpp_hbm, layer, wpp_ref, PLE_DIM, 0, D_MODEL)
                + _weight_jobs(w_up_hbm, layer, wv_ref, D_MODEL, 0, D_FF)
                + _weight_jobs(w_up_hbm, layer, wgt_ref, D_MODEL, D_FF, D_FF)
                + _weight_jobs(wd_hbm, layer, wd_ref, D_FF, 0, D_MODEL))
        _load_weights_as_bf16(jobs, stage_sc, sem)

    tiles_per_seq = SEQ // FFN_TILE
    pos = i % tiles_per_seq
    keep_prev = (pos != 0).astype(f32)
    keep_next = (pos != tiles_per_seq - 1).astype(f32)

    h = h_ref[...]
    hb = h.astype(bf16)
    main = slice(FFN_HALO, FFN_HALO + FFN_TILE)
    hext_sc[main, :] = hb
    hext_sc[0:FFN_HALO, :] = (hprev_ref[...] * keep_prev).astype(bf16)
    hext_sc[FFN_HALO + FFN_TILE:, :] = (hnext_ref[...] * keep_next).astype(bf16)

    ple = jax.nn.sigmoid(_dot(hb, wpg_ref[...])) * _dot(p_ref[...].astype(bf16), wpp_ref[...])
    acc_sc[...] = ALPHA * h + ple

    def up(c, slot):
        cols = slice(c * FF_CHUNK, (c + 1) * FF_CHUNK)
        val_sc[slot] = _dot(hext_sc[main, :], wv_ref[:, cols])
        gate_sc[slot] = _dot(hext_sc[...], wgt_ref[:, cols])

    def activate(c, slot):
        cols = slice(c * FF_CHUNK, (c + 1) * FF_CHUNK)
        gate = gate_sc[slot]
        n = FFN_TILE + 2 * FFN_HALO
        prev = pltpu.roll(gate, 1, 0)[main]
        nxt = pltpu.roll(gate, n - 1, 0)[main]
        y = prev * cw_ref[0:1, cols] + gate[main] * cw_ref[1:2, cols] + nxt * cw_ref[2:3, cols] + cb_ref[:, cols]
        act = 0.5 * y * (1.0 + lax.erf(y * np.float32(np.sqrt(0.5))))
        half = slice((c % 2) * FF_CHUNK, (c % 2 + 1) * FF_CHUNK)
        act_sc[(c // 2) % 2, :, half] = (act * val_sc[slot]).astype(bf16)

    def down(c0, n_chunks):
        k = n_chunks * FF_CHUNK
        acc_sc[...] += _dot(act_sc[(c0 // 2) % 2, :, 0:k], wd_ref[c0 * FF_CHUNK:c0 * FF_CHUNK + k, :])

    for it in range(N_FF_CHUNKS + 2):
        if it < N_FF_CHUNKS:
            up(it, it % 2)
        if 1 <= it <= N_FF_CHUNKS:
            activate(it - 1, (it - 1) % 2)
        c = it - 2
        if c >= 0 and c % 2 == 1:
            down(c - 1, 2)
        elif c == N_FF_CHUNKS - 1:
            down(c, 1)
    o_ref[...] = _layer_norm(acc_sc[...], g_ref[...], b_ref[...])


def _ffn(layer, h2d, p3d, w_up, conv_w, conv_b, w_down, w_ple_gate, w_ple_proj, ln_g, ln_b):
    halo_blocks = FFN_TILE // FFN_HALO
    n_halo_blocks = TOKENS // FFN_HALO
    grid_spec = pltpu.PrefetchScalarGridSpec(
        num_scalar_prefetch=1,
        grid=(TOKENS // FFN_TILE,),
        in_specs=[
            pl.BlockSpec((FFN_TILE, D_MODEL), lambda i, l: (i, 0)),
            pl.BlockSpec((FFN_HALO, D_MODEL), lambda i, l: (jnp.maximum(i * halo_blocks - 1, 0), 0)),
            pl.BlockSpec((FFN_HALO, D_MODEL),
                         lambda i, l: (jnp.minimum((i + 1) * halo_blocks, n_halo_blocks - 1), 0)),
            pl.BlockSpec((None, FFN_TILE, PLE_DIM), lambda i, l: (l[0], i, 0)),
            _HBM,
            _layer_block((3, D_FF)),
            _layer_block((1, D_FF)),
            _HBM, _HBM, _HBM,
            _layer_block((1, D_MODEL)), _layer_block((1, D_MODEL)),
        ],
        out_specs=pl.BlockSpec((FFN_TILE, D_MODEL), lambda i, l: (i, 0)),
        scratch_shapes=[
            pltpu.VMEM((FFN_TILE + 2 * FFN_HALO, D_MODEL), bf16),
            pltpu.VMEM((2, FFN_TILE, FF_CHUNK), f32),
            pltpu.VMEM((2, FFN_TILE + 2 * FFN_HALO, FF_CHUNK), f32),
            pltpu.VMEM((2, FFN_TILE, 2 * FF_CHUNK), bf16),
            pltpu.VMEM((FFN_TILE, D_MODEL), f32),
            pltpu.VMEM((D_MODEL, D_FF), bf16),
            pltpu.VMEM((D_MODEL, D_FF), bf16),
            pltpu.VMEM((D_FF, D_MODEL), bf16),
            pltpu.VMEM((D_MODEL, D_MODEL), bf16),
            pltpu.VMEM((PLE_DIM, D_MODEL), bf16),
            pltpu.VMEM((2, STAGE_ROWS, D_FF), f32),
            pltpu.SemaphoreType.DMA((2,)),
        ],
    )
    return pl.pallas_call(
        _ffn_kernel,
        grid_spec=grid_spec,
        out_shape=jax.ShapeDtypeStruct((TOKENS, D_MODEL), f32),
        compiler_params=pltpu.CompilerParams(
            dimension_semantics=("arbitrary",), vmem_limit_bytes=VMEM_LIMIT),
        name="ffn",
    )(layer, h2d, h2d, h2d, p3d, w_up, conv_w, conv_b, w_down, w_ple_gate, w_ple_proj, ln_g, ln_b)


def _column_geometry():
    qc = np.arange(GRID_W)[:, None]
    kc = np.arange(GRID_W)[None, :]
    col_start = np.clip(qc - KW // 2, 0, GRID_W - KW)
    valid = (kc >= col_start) & (kc < col_start + KW)
    col_off = np.clip(kc - qc, -(KW - 1), KW - 1) + KW - 1
    onehot = (col_off[..., None] == np.arange(2 * KW - 1)).astype(np.float32)
    valid_tbl = np.tile(valid.astype(np.float32), (2, 2))
    return onehot, valid_tbl


def _bias_tables(rpb):
    onehot, _ = _column_geometry()
    select = np.einsum('ij,qkc->qikjc', np.eye(2, dtype=np.float32), onehot)
    select = select.reshape(GRID_W, LANES, 2, 2 * KW - 1)
    two_rows = jnp.stack([rpb[:, :, :-1], rpb[:, :, 1:]], axis=3)
    tbl = jnp.einsum('lhrjc,qmjc->lrhqm', two_rows, jnp.asarray(select),
                     precision=lax.Precision.HIGHEST)
    return tbl.reshape(DEPTH, 2 * KH - 2, N_PAIRS, 2 * GRID_W, LANES)


def kernel(x, p, ln_in_g, ln_in_b, w_in, b_in, rpb, w_attn_out, pool_w, pool_scale, w_pool_out,
           w_mix_out, ln1_g, ln1_b, w_up, conv_w, conv_b, w_down, w_ple_gate, w_ple_proj,
           ln2_g, ln2_b):
    valid_tbl = jnp.asarray(_column_geometry()[1])
    bias_tbl = _bias_tables(rpb)
    p3d = p.reshape(DEPTH, TOKENS, PLE_DIM)
    row = lambda a: a.reshape(1, -1)
    rows = lambda a: a.reshape(DEPTH, 1, -1)
    pool_w_b = pool_w.astype(bf16)
    b_in_r, pool_scale_r, conv_b_r = rows(b_in), rows(pool_scale), rows(conv_b)
    ln1_g_r, ln1_b_r, ln2_g_r, ln2_b_r = rows(ln1_g), rows(ln1_b), rows(ln2_g), rows(ln2_b)

    h = _input_ln(x.reshape(TOKENS, D_MODEL), row(ln_in_g), row(ln_in_b))
    for i in range(DEPTH):
        layer = jnp.full((1,), i, jnp.int32)
        h3d = h.reshape(BATCH, SEQ, D_MODEL)
        qlo, qhi, k, v, pooled = _qkvu(layer, h3d, w_in, b_in_r, pool_w_b, pool_scale_r)
        h3d = _mix(layer, h3d, qlo, qhi, k, v, pooled, bias_tbl, valid_tbl,
                   w_in, b_in_r, w_attn_out, w_pool_out, w_mix_out, ln1_g_r, ln1_b_r)
        h = _ffn(layer, h3d.reshape(TOKENS, D_MODEL), p3d, w_up, conv_w, conv_b_r, w_down,
                 w_ple_gate, w_ple_proj, ln2_g_r, ln2_b_r)
    return h.reshape(BATCH, SEQ, D_MODEL)
```

```python
import functools

import numpy as np
import jax
import jax.numpy as jnp
from jax import lax
from jax.experimental import pallas as pl
from jax.experimental.pallas import tpu as pltpu

D_MODEL = 1024
BATCH = 8
SEQ = 2048
DEPTH = 4
GRID_W = 64
ROWS = SEQ // GRID_W
N_HEADS = 8
HEAD_DIM = 64
D_ATTN = N_HEADS * HEAD_DIM
N_PAIRS = N_HEADS // 2
KH = 8
KW = 16
WIN_KEYS = KH * GRID_W
POOL_WINDOWS = (2, 4, 8, 16)
D_POOL = 512
POOL_GROUP_DIM = 128
D_FF = 2816
FF_CHUNK = 256
N_FF_CHUNKS = D_FF // FF_CHUNK
PLE_DIM = 256
ALPHA = (2 * DEPTH) ** 0.25
LN_EPS = 1e-5
NEG_INF = -1e30
QK_SCALE = HEAD_DIM ** -0.5

LANES = 128
BF16_SUBLANES = 16
VMEM_LIMIT = 56 * 1024 * 1024

TOKENS = BATCH * SEQ
LN_TILE = 1024
POOL_CHUNK = 256
POOL_HALO = 16
MIX_ROWS = 8
MIX_TILE = MIX_ROWS * GRID_W
MIX_DENSE_PARTS = 2
FFN_HALO = BF16_SUBLANES

f32 = jnp.float32
bf16 = jnp.bfloat16


def _layer_norm(xf, g, b):
    mu = jnp.mean(xf, axis=-1, keepdims=True)
    xc = xf - mu
    var = jnp.mean(xc * xc, axis=-1, keepdims=True)
    return xc * lax.rsqrt(var + LN_EPS) * g + b


def _dot(a, b):
    return jnp.dot(a, b, preferred_element_type=f32)


def _layer_block(shape, col_block=0, single_buffer=False):
    def index_map(*args):
        layer = args[-1][0]
        return (layer,) + (0,) * (len(shape) - 1) + (col_block,)
    mode = dict(pipeline_mode=pl.Buffered(1)) if single_buffer else {}
    return pl.BlockSpec((None,) + tuple(shape), index_map, **mode)


def _ln_kernel(x_ref, g_ref, b_ref, o_ref):
    o_ref[...] = _layer_norm(x_ref[...], g_ref[...], b_ref[...])


def _input_ln(x2d, g, b):
    return pl.pallas_call(
        _ln_kernel,
        grid=(TOKENS // LN_TILE,),
        in_specs=[
            pl.BlockSpec((LN_TILE, D_MODEL), lambda i: (i, 0)),
            pl.BlockSpec((1, D_MODEL), lambda i: (0, 0)),
            pl.BlockSpec((1, D_MODEL), lambda i: (0, 0)),
        ],
        out_specs=pl.BlockSpec((LN_TILE, D_MODEL), lambda i: (i, 0)),
        out_shape=jax.ShapeDtypeStruct((TOKENS, D_MODEL), f32),
        compiler_params=pltpu.CompilerParams(
            dimension_semantics=("arbitrary",), vmem_limit_bytes=VMEM_LIMIT),
        name="input_ln",
    )(x2d, g, b)


def _qkvu_kernel(layer_ref, h_ref, w_ref, b_ref, pw_ref, ps_ref,
                 qlo_ref, qhi_ref, k_ref, v_ref, pooled_ref,
                 upad_sc, pd_sc, *, QKVU_CHUNK, pool_doubling):
    del layer_ref
    lane = lax.broadcasted_iota(jnp.int32, (QKVU_CHUNK, D_ATTN), 1)
    even_head = (lane % LANES) < HEAD_DIM

    zeros_halo = jnp.zeros((POOL_HALO, D_POOL), f32)
    upad_sc[0:POOL_HALO, :] = zeros_halo
    upad_sc[POOL_HALO + SEQ:POOL_HALO + SEQ + POOL_HALO, :] = zeros_halo

    for c in range(SEQ // QKVU_CHUNK):
        rows = slice(c * QKVU_CHUNK, (c + 1) * QKVU_CHUNK)
        hb = h_ref[rows, :].astype(bf16)
        q = (_dot(hb, w_ref[:, 0:D_ATTN]) + b_ref[:, 0:D_ATTN]) * QK_SCALE
        qlo_ref[rows, :] = jnp.where(even_head, q, 0.0).astype(bf16)
        qhi_ref[rows, :] = jnp.where(even_head, 0.0, q).astype(bf16)
        k = _dot(hb, w_ref[:, D_ATTN:2 * D_ATTN]) + b_ref[:, D_ATTN:2 * D_ATTN]
        k_ref[rows, :] = k.astype(bf16)
        v = _dot(hb, w_ref[:, 2 * D_ATTN:3 * D_ATTN]) + b_ref[:, 2 * D_ATTN:3 * D_ATTN]
        v_ref[rows, :] = v.astype(bf16)
        u = _dot(hb, w_ref[:, 3 * D_ATTN:3 * D_ATTN + D_POOL]) + b_ref[:, 3 * D_ATTN:3 * D_ATTN + D_POOL]
        upad_sc[POOL_HALO + c * QKVU_CHUNK:POOL_HALO + (c + 1) * QKVU_CHUNK, :] = u

    if pool_doubling:
        n_pad = SEQ + 2 * POOL_HALO
        seq_rows = slice(POOL_HALO, POOL_HALO + SEQ)
        t = lax.broadcasted_iota(jnp.int32, (SEQ, 1), 0)
        for g, w in enumerate(POOL_WINDOWS):
            half = w // 2
            cols = slice(g * POOL_GROUP_DIM, (g + 1) * POOL_GROUP_DIM)
            u = upad_sc[:, cols]
            win = pltpu.roll(u, 1, 0) + u
            reach = 1
            for _ in range(g):
                win = pltpu.roll(win, reach, 0) + pltpu.roll(win, n_pad - reach, 0)
                reach *= 2
            cnt = (jnp.minimum(t + half, SEQ) - jnp.maximum(t - half, 0)).astype(f32)
            pd_sc[:, cols] = (win[seq_rows] / cnt - u[seq_rows]).astype(bf16)
    else:
        for c in range(SEQ // POOL_CHUNK):
            t = lax.broadcasted_iota(jnp.int32, (POOL_CHUNK, 1), 0) + c * POOL_CHUNK
            base = POOL_HALO + c * POOL_CHUNK
            for g, w in enumerate(POOL_WINDOWS):
                half = w // 2
                cols = slice(g * POOL_GROUP_DIM, (g + 1) * POOL_GROUP_DIM)
                acc = upad_sc[base - half:base - half + POOL_CHUNK, cols]
                for j in range(-half + 1, half):
                    acc = acc + upad_sc[base + j:base + j + POOL_CHUNK, cols]
                cnt = (jnp.minimum(t + half, SEQ) - jnp.maximum(t - half, 0)).astype(f32)
                centre = upad_sc[base:base + POOL_CHUNK, cols]
                pd_sc[c * POOL_CHUNK:(c + 1) * POOL_CHUNK, cols] = (acc / cnt - centre).astype(bf16)

    for g in range(len(POOL_WINDOWS)):
        cols = slice(g * POOL_GROUP_DIM, (g + 1) * POOL_GROUP_DIM)
        pooled = _dot(pd_sc[:, cols], pw_ref[g]) * ps_ref[:, cols]
        pooled_ref[:, cols] = pooled.astype(bf16)


def _qkvu(layer, h3d, w_in, b_in, pool_w, pool_scale, **variant):
    n_qkvu = 3 * D_ATTN + D_POOL
    seq_spec = lambda width: pl.BlockSpec((None, SEQ, width), lambda b, l: (b, 0, 0))
    out_sds = jax.ShapeDtypeStruct((BATCH, SEQ, D_ATTN), bf16)
    grid_spec = pltpu.PrefetchScalarGridSpec(
        num_scalar_prefetch=1,
        grid=(BATCH,),
        in_specs=[
            seq_spec(D_MODEL),
            _layer_block((D_MODEL, n_qkvu), single_buffer=True),
            _layer_block((1, n_qkvu)),
            _layer_block((len(POOL_WINDOWS), POOL_GROUP_DIM, POOL_GROUP_DIM)),
            _layer_block((1, D_POOL)),
        ],
        out_specs=[seq_spec(D_ATTN)] * 5,
        scratch_shapes=[
            pltpu.VMEM((SEQ + 2 * POOL_HALO, D_POOL), f32),
            pltpu.VMEM((SEQ, D_POOL), bf16),
        ],
    )
    return pl.pallas_call(
        functools.partial(_qkvu_kernel, **variant),
        grid_spec=grid_spec,
        out_shape=[out_sds] * 5,
        compiler_params=pltpu.CompilerParams(
            dimension_semantics=("arbitrary",), vmem_limit_bytes=VMEM_LIMIT),
        name="qkvu",
    )(layer, h3d, w_in, b_in, pool_w, pool_scale)


def _mix_kernel(layer_ref, h_ref, qlo_ref, qhi_ref, k_ref, v_ref, pooled_ref, bias_ref, valid_ref,
                wg_ref, bg_ref, wa_ref, wp_ref, wm_ref, g_ref, b_ref,
                o_ref, attn_sc, e_sc, l_sc):
    del layer_ref
    c = pl.program_id(1)
    even_head = lax.broadcasted_iota(jnp.int32, (GRID_W, LANES), 1) < HEAD_DIM

    def window(rr):
        r = c * MIX_ROWS + rr
        rs = jnp.clip(r - KH // 2, 0, ROWS - KH)
        return r - rs, pl.multiple_of(rs * GRID_W, GRID_W)

    def scores_softmax(rr, slot):
        delta, k0 = window(rr)
        rows = slice(rr * GRID_W, (rr + 1) * GRID_W)
        valid = valid_ref[...] > 0.5
        for p in range(N_PAIRS):
            cols = slice(p * LANES, (p + 1) * LANES)
            lhs = jnp.concatenate([qlo_ref[rows, cols], qhi_ref[rows, cols]], axis=0)
            kw = k_ref[pl.ds(k0, WIN_KEYS), cols]
            s = lax.dot_general(lhs, kw, (((1,), (1,)), ((), ())),
                                preferred_element_type=f32)
            parts = []
            for j in range(WIN_KEYS // LANES):
                sj = s[:, j * LANES:(j + 1) * LANES] + bias_ref[2 * j + KH - 1 - delta, p]
                parts.append(jnp.where(valid, sj, NEG_INF))
            m = functools.reduce(jnp.maximum, parts)
            m = jnp.max(m, axis=-1, keepdims=True)
            es = [jnp.exp(sj - m) for sj in parts]
            l_sc[slot, p] = jnp.sum(functools.reduce(lambda x, y: x + y, es), axis=-1, keepdims=True)
            for j, ej in enumerate(es):
                e_sc[slot, p, :, j * LANES:(j + 1) * LANES] = ej.astype(bf16)

    def weighted_values(rr, slot):
        _, k0 = window(rr)
        rows = slice(rr * GRID_W, (rr + 1) * GRID_W)
        for p in range(N_PAIRS):
            cols = slice(p * LANES, (p + 1) * LANES)
            vw = v_ref[pl.ds(k0, WIN_KEYS), cols]
            o = _dot(e_sc[slot, p], vw) / l_sc[slot, p]
            o = jnp.where(even_head, o[:GRID_W], o[GRID_W:])
            attn_sc[rows, cols] = o.astype(bf16)

    for it in range(MIX_ROWS + 1):
        if it < MIX_ROWS:
            scores_softmax(it, it % 2)
        if it >= 1:
            weighted_values(it - 1, (it - 1) % 2)

    part_rows = MIX_TILE // MIX_DENSE_PARTS
    for part in range(MIX_DENSE_PARTS):
        rows = slice(part * part_rows, (part + 1) * part_rows)
        h = h_ref[rows, :]
        hb = h.astype(bf16)
        ga = jax.nn.sigmoid(_dot(hb, wg_ref[:, 0:D_MODEL]) + bg_ref[:, 0:D_MODEL])
        merged = ga * _dot(attn_sc[rows, :], wa_ref[...])
        gb = jax.nn.sigmoid(_dot(hb, wg_ref[:, D_MODEL:2 * D_MODEL]) + bg_ref[:, D_MODEL:2 * D_MODEL])
        merged = merged + gb * _dot(pooled_ref[rows, :], wp_ref[...])
        z = ALPHA * h + _dot(merged.astype(bf16), wm_ref[...])
        o_ref[rows, :] = _layer_norm(z, g_ref[...], b_ref[...])


def _mix(layer, h3d, qlo, qhi, k, v, pooled, bias_tbl, valid_tbl,
         w_in, b_in, w_attn_out, w_pool_out, w_mix_out, ln_g, ln_b):
    tile = lambda width: pl.BlockSpec((None, MIX_TILE, width), lambda b, c, l: (b, c, 0))
    seq = lambda width: pl.BlockSpec((None, SEQ, width), lambda b, c, l: (b, 0, 0))
    grid_spec = pltpu.PrefetchScalarGridSpec(
        num_scalar_prefetch=1,
        grid=(BATCH, SEQ // MIX_TILE),
        in_specs=[
            tile(D_MODEL), tile(D_ATTN), tile(D_ATTN), seq(D_ATTN), seq(D_ATTN), tile(D_POOL),
            _layer_block((2 * KH - 2, N_PAIRS, 2 * GRID_W, LANES), single_buffer=True),
            pl.BlockSpec((2 * GRID_W, LANES), lambda b, c, l: (0, 0)),
            _layer_block((D_MODEL, 2 * D_MODEL), col_block=1, single_buffer=True),
            _layer_block((1, 2 * D_MODEL), col_block=1),
            _layer_block((D_ATTN, D_MODEL), single_buffer=True),
            _layer_block((D_POOL, D_MODEL), single_buffer=True),
            _layer_block((D_MODEL, D_MODEL), single_buffer=True),
            _layer_block((1, D_MODEL)), _layer_block((1, D_MODEL)),
        ],
        out_specs=tile(D_MODEL),
        scratch_shapes=[
            pltpu.VMEM((MIX_TILE, D_ATTN), bf16),
            pltpu.VMEM((2, N_PAIRS, 2 * GRID_W, WIN_KEYS), bf16),
            pltpu.VMEM((2, N_PAIRS, 2 * GRID_W, 1), f32),
        ],
    )
    return pl.pallas_call(
        _mix_kernel,
        grid_spec=grid_spec,
        out_shape=jax.ShapeDtypeStruct((BATCH, SEQ, D_MODEL), f32),
        compiler_params=pltpu.CompilerParams(
            dimension_semantics=("arbitrary", "arbitrary"), vmem_limit_bytes=VMEM_LIMIT),
        name="mix",
    )(layer, h3d, qlo, qhi, k, v, pooled, bias_tbl, valid_tbl,
      w_in, b_in, w_attn_out, w_pool_out, w_mix_out, ln_g, ln_b)


def _ffn_kernel(layer_ref, h_ref, hprev_ref, hnext_ref, p_ref,
                wv_ref, wgt_ref, cw_ref, cb_ref, wd_ref, wpg_ref, wpp_ref, g_ref, b_ref,
                o_ref, hext_sc, val_sc, gate_sc, act_sc, acc_sc, *, FFN_TILE, wide):
    del layer_ref
    i = pl.program_id(0)
    tiles_per_seq = SEQ // FFN_TILE
    pos = i % tiles_per_seq
    keep_prev = (pos != 0).astype(f32)
    keep_next = (pos != tiles_per_seq - 1).astype(f32)

    h = h_ref[...]
    hb = h.astype(bf16)
    main = slice(FFN_HALO, FFN_HALO + FFN_TILE)
    hext_sc[main, :] = hb
    hext_sc[0:FFN_HALO, :] = (hprev_ref[...] * keep_prev).astype(bf16)
    hext_sc[FFN_HALO + FFN_TILE:, :] = (hnext_ref[...] * keep_next).astype(bf16)

    ple = jax.nn.sigmoid(_dot(hb, wpg_ref[...])) * _dot(p_ref[...].astype(bf16), wpp_ref[...])
    acc_sc[...] = ALPHA * h + ple

    width = 2 * FF_CHUNK if wide else FF_CHUNK
    chunks = [(c0, min(width, D_FF - c0)) for c0 in range(0, D_FF, width)]

    def up(c, slot):
        c0, w = chunks[c]
        cols = slice(c0, c0 + w)
        val_sc[slot, :, 0:w] = _dot(hext_sc[main, :], wv_ref[:, cols])
        gate_sc[slot, :, 0:w] = _dot(hext_sc[...], wgt_ref[:, cols])

    def activate(c, slot):
        c0, w = chunks[c]
        cols = slice(c0, c0 + w)
        gate = gate_sc[slot, :, 0:w]
        n = FFN_TILE + 2 * FFN_HALO
        prev = pltpu.roll(gate, 1, 0)[main]
        nxt = pltpu.roll(gate, n - 1, 0)[main]
        y = prev * cw_ref[0:1, cols] + gate[main] * cw_ref[1:2, cols] + nxt * cw_ref[2:3, cols] + cb_ref[:, cols]
        act = 0.5 * y * (1.0 + lax.erf(y * np.float32(np.sqrt(0.5))))
        a = (act * val_sc[slot, :, 0:w]).astype(bf16)
        if wide:
            act_sc[slot, :, 0:w] = a
        else:
            act_sc[(c // 2) % 2, :, (c % 2) * FF_CHUNK:(c % 2 + 1) * FF_CHUNK] = a

    def down(aslot, c0, k):
        acc_sc[...] += _dot(act_sc[aslot, :, 0:k], wd_ref[c0:c0 + k, :])

    n_chunks = len(chunks)
    for it in range(n_chunks + 2):
        if it < n_chunks:
            up(it, it % 2)
        if 1 <= it <= n_chunks:
            activate(it - 1, (it - 1) % 2)
        c = it - 2
        if c < 0:
            continue
        if wide:
            down(c % 2, *chunks[c])
        elif c % 2 == 1:
            down((c // 2) % 2, chunks[c - 1][0], 2 * FF_CHUNK)
        elif c == n_chunks - 1:
            down((c // 2) % 2, chunks[c][0], FF_CHUNK)
    o_ref[...] = _layer_norm(acc_sc[...], g_ref[...], b_ref[...])


def _ffn(layer, h2d, p3d, w_up, conv_w, conv_b, w_down, w_ple_gate, w_ple_proj, ln_g, ln_b, *, FFN_TILE, wide):
    halo_blocks = FFN_TILE // FFN_HALO
    n_halo_blocks = TOKENS // FFN_HALO
    grid_spec = pltpu.PrefetchScalarGridSpec(
        num_scalar_prefetch=1,
        grid=(TOKENS // FFN_TILE,),
        in_specs=[
            pl.BlockSpec((FFN_TILE, D_MODEL), lambda i, l: (i, 0)),
            pl.BlockSpec((FFN_HALO, D_MODEL), lambda i, l: (jnp.maximum(i * halo_blocks - 1, 0), 0)),
            pl.BlockSpec((FFN_HALO, D_MODEL),
                         lambda i, l: (jnp.minimum((i + 1) * halo_blocks, n_halo_blocks - 1), 0)),
            pl.BlockSpec((None, FFN_TILE, PLE_DIM), lambda i, l: (l[0], i, 0)),
            _layer_block((D_MODEL, D_FF), col_block=0, single_buffer=True),
            _layer_block((D_MODEL, D_FF), col_block=1, single_buffer=True),
            _layer_block((3, D_FF)),
            _layer_block((1, D_FF)),
            _layer_block((D_FF, D_MODEL), single_buffer=True),
            _layer_block((D_MODEL, D_MODEL), single_buffer=True),
            _layer_block((PLE_DIM, D_MODEL), single_buffer=True),
            _layer_block((1, D_MODEL)), _layer_block((1, D_MODEL)),
        ],
        out_specs=pl.BlockSpec((FFN_TILE, D_MODEL), lambda i, l: (i, 0)),
        scratch_shapes=[
            pltpu.VMEM((FFN_TILE + 2 * FFN_HALO, D_MODEL), bf16),
            pltpu.VMEM((2, FFN_TILE, 2 * FF_CHUNK if wide else FF_CHUNK), f32),
            pltpu.VMEM((2, FFN_TILE + 2 * FFN_HALO, 2 * FF_CHUNK if wide else FF_CHUNK), f32),
            pltpu.VMEM((2, FFN_TILE, 2 * FF_CHUNK), bf16),
            pltpu.VMEM((FFN_TILE, D_MODEL), f32),
        ],
    )
    return pl.pallas_call(
        functools.partial(_ffn_kernel, FFN_TILE=FFN_TILE, wide=wide),
        grid_spec=grid_spec,
        out_shape=jax.ShapeDtypeStruct((TOKENS, D_MODEL), f32),
        compiler_params=pltpu.CompilerParams(
            dimension_semantics=("arbitrary",), vmem_limit_bytes=VMEM_LIMIT),
        name="ffn",
    )(layer, h2d, h2d, h2d, p3d, w_up, w_up, conv_w, conv_b, w_down, w_ple_gate, w_ple_proj, ln_g, ln_b)


def _column_geometry():
    qc = np.arange(GRID_W)[:, None]
    kc = np.arange(GRID_W)[None, :]
    col_start = np.clip(qc - KW // 2, 0, GRID_W - KW)
    valid = (kc >= col_start) & (kc < col_start + KW)
    col_off = np.clip(kc - qc, -(KW - 1), KW - 1) + KW - 1
    onehot = (col_off[..., None] == np.arange(2 * KW - 1)).astype(np.float32)
    valid_tbl = np.tile(valid.astype(np.float32), (2, 2))
    return onehot, valid_tbl


def _bias_tables(rpb):
    onehot, _ = _column_geometry()
    select = np.einsum('ij,qkc->qikjc', np.eye(2, dtype=np.float32), onehot)
    select = select.reshape(GRID_W, LANES, 2, 2 * KW - 1)
    two_rows = jnp.stack([rpb[:, :, :-1], rpb[:, :, 1:]], axis=3)
    tbl = jnp.einsum('lhrjc,qmjc->lrhqm', two_rows, jnp.asarray(select),
                     precision=lax.Precision.HIGHEST)
    return tbl.reshape(DEPTH, 2 * KH - 2, N_PAIRS, 2 * GRID_W, LANES)


QKVU_VARIANTS = (
    dict(QKVU_CHUNK=512, pool_doubling=False),
    dict(QKVU_CHUNK=512, pool_doubling=True),
    dict(QKVU_CHUNK=1024, pool_doubling=False),
    dict(QKVU_CHUNK=1024, pool_doubling=True),
)
FFN_VARIANTS = (
    dict(FFN_TILE=512, wide=False),
    dict(FFN_TILE=512, wide=True),
    dict(FFN_TILE=1024, wide=False),
    dict(FFN_TILE=512, wide=True),
)


def kernel(x, p, ln_in_g, ln_in_b, w_in, b_in, rpb, w_attn_out, pool_w, pool_scale, w_pool_out,
           w_mix_out, ln1_g, ln1_b, w_up, conv_w, conv_b, w_down, w_ple_gate, w_ple_proj,
           ln2_g, ln2_b):
    valid_tbl = jnp.asarray(_column_geometry()[1])
    bias_tbl = _bias_tables(rpb)
    p3d = p.reshape(DEPTH, TOKENS, PLE_DIM)
    row = lambda a: a.reshape(1, -1)
    rows = lambda a: a.reshape(DEPTH, 1, -1)
    w_in_b, w_up_b, w_down_b = w_in.astype(bf16), w_up.astype(bf16), w_down.astype(bf16)
    w_attn_out_b, w_pool_out_b, w_mix_out_b = (w.astype(bf16) for w in (w_attn_out, w_pool_out, w_mix_out))
    pool_w_b, w_ple_gate_b, w_ple_proj_b = (w.astype(bf16) for w in (pool_w, w_ple_gate, w_ple_proj))
    b_in_r, pool_scale_r, conv_b_r = rows(b_in), rows(pool_scale), rows(conv_b)
    ln1_g_r, ln1_b_r, ln2_g_r, ln2_b_r = rows(ln1_g), rows(ln1_b), rows(ln2_g), rows(ln2_b)

    h = _input_ln(x.reshape(TOKENS, D_MODEL), row(ln_in_g), row(ln_in_b))
    for i in range(DEPTH):
        layer = jnp.full((1,), i, jnp.int32)
        h3d = h.reshape(BATCH, SEQ, D_MODEL)
        qlo, qhi, k, v, pooled = _qkvu(layer, h3d, w_in_b, b_in_r, pool_w_b, pool_scale_r, **QKVU_VARIANTS[i])
        h3d = _mix(layer, h3d, qlo, qhi, k, v, pooled, bias_tbl, valid_tbl,
                   w_in_b, b_in_r, w_attn_out_b, w_pool_out_b, w_mix_out_b, ln1_g_r, ln1_b_r)
        h = _ffn(layer, h3d.reshape(TOKENS, D_MODEL), p3d, w_up_b, conv_w, conv_b_r, w_down_b,
                 w_ple_gate_b, w_ple_proj_b, ln2_g_r, ln2_b_r, **FFN_VARIANTS[i])
    return h.reshape(BATCH, SEQ, D_MODEL)
```

```python
import functools

import numpy as np
import jax
import jax.numpy as jnp
from jax import lax
from jax.experimental import pallas as pl
from jax.experimental.pallas import tpu as pltpu

D_MODEL = 1024
BATCH = 8
SEQ = 2048
DEPTH = 4
GRID_W = 64
ROWS = SEQ // GRID_W
N_HEADS = 8
HEAD_DIM = 64
D_ATTN = N_HEADS * HEAD_DIM
N_PAIRS = N_HEADS // 2
KH = 8
KW = 16
WIN_KEYS = KH * GRID_W
POOL_WINDOWS = (2, 4, 8, 16)
D_POOL = 512
POOL_GROUP_DIM = 128
D_FF = 2816
FF_CHUNK = 256
N_FF_CHUNKS = D_FF // FF_CHUNK
PLE_DIM = 256
ALPHA = (2 * DEPTH) ** 0.25
LN_EPS = 1e-5
NEG_INF = -1e30
QK_SCALE = HEAD_DIM ** -0.5

LANES = 128
BF16_SUBLANES = 16
VMEM_LIMIT = 56 * 1024 * 1024

TOKENS = BATCH * SEQ
LN_TILE = 1024
QKVU_CHUNK = 512
POOL_CHUNK = 256
POOL_HALO = 16
MIX_ROWS = 8
MIX_TILE = MIX_ROWS * GRID_W
MIX_DENSE_PARTS = 2
FFN_TILE = 512
FFN_HALO = BF16_SUBLANES
STAGE_ROWS = 256
STAGE_SLOTS = 3

f32 = jnp.float32
bf16 = jnp.bfloat16


def _layer_norm(xf, g, b):
    mu = jnp.mean(xf, axis=-1, keepdims=True)
    xc = xf - mu
    var = jnp.mean(xc * xc, axis=-1, keepdims=True)
    return xc * lax.rsqrt(var + LN_EPS) * g + b


def _dot(a, b):
    return jnp.dot(a, b, preferred_element_type=f32)


def _layer_block(shape, col_block=0, single_buffer=False):
    def index_map(*args):
        layer = args[-1][0]
        return (layer,) + (0,) * (len(shape) - 1) + (col_block,)
    mode = dict(pipeline_mode=pl.Buffered(1)) if single_buffer else {}
    return pl.BlockSpec((None,) + tuple(shape), index_map, **mode)


def _load_weights_as_bf16(jobs, stage_sc, sem):
    def copy(j):
        src, _, _, rows, cols = jobs[j]
        slot = j % STAGE_SLOTS
        return pltpu.make_async_copy(src, stage_sc.at[slot, pl.ds(0, rows), pl.ds(0, cols)], sem.at[slot])

    for j in range(min(STAGE_SLOTS - 1, len(jobs))):
        copy(j).start()
    for j, (_, dst, row0, rows, cols) in enumerate(jobs):
        if j + STAGE_SLOTS - 1 < len(jobs):
            copy(j + STAGE_SLOTS - 1).start()
        copy(j).wait()
        dst[row0:row0 + rows, :] = stage_sc[j % STAGE_SLOTS, 0:rows, 0:cols].astype(bf16)


def _weight_jobs(w_hbm, layer, dst, n_rows, col0, cols):
    return [(w_hbm.at[layer, pl.ds(r0, min(STAGE_ROWS, n_rows - r0)), pl.ds(col0, cols)],
             dst, r0, min(STAGE_ROWS, n_rows - r0), cols)
            for r0 in range(0, n_rows, STAGE_ROWS)]


_HBM = pl.BlockSpec(memory_space=pl.ANY)


def _ln_kernel(x_ref, g_ref, b_ref, o_ref):
    o_ref[...] = _layer_norm(x_ref[...], g_ref[...], b_ref[...])


def _input_ln(x2d, g, b):
    return pl.pallas_call(
        _ln_kernel,
        grid=(TOKENS // LN_TILE,),
        in_specs=[
            pl.BlockSpec((LN_TILE, D_MODEL), lambda i: (i, 0)),
            pl.BlockSpec((1, D_MODEL), lambda i: (0, 0)),
            pl.BlockSpec((1, D_MODEL), lambda i: (0, 0)),
        ],
        out_specs=pl.BlockSpec((LN_TILE, D_MODEL), lambda i: (i, 0)),
        out_shape=jax.ShapeDtypeStruct((TOKENS, D_MODEL), f32),
        compiler_params=pltpu.CompilerParams(
            dimension_semantics=("arbitrary",), vmem_limit_bytes=VMEM_LIMIT),
        name="input_ln",
    )(x2d, g, b)


def _qkvu_kernel(layer_ref, h_ref, w_hbm, b_ref, pw_ref, ps_ref,
                 qlo_ref, qhi_ref, k_ref, v_ref, pooled_ref,
                 upad_sc, pd_sc, w_ref, stage_sc, sem):
    n_qkvu = 3 * D_ATTN + D_POOL

    @pl.when(pl.program_id(0) == 0)
    def _():
        _load_weights_as_bf16(_weight_jobs(w_hbm, layer_ref[0], w_ref, D_MODEL, 0, n_qkvu), stage_sc, sem)

    lane = lax.broadcasted_iota(jnp.int32, (QKVU_CHUNK, D_ATTN), 1)
    even_head = (lane % LANES) < HEAD_DIM

    zeros_halo = jnp.zeros((POOL_HALO, D_POOL), f32)
    upad_sc[0:POOL_HALO, :] = zeros_halo
    upad_sc[POOL_HALO + SEQ:POOL_HALO + SEQ + POOL_HALO, :] = zeros_halo

    for c in range(SEQ // QKVU_CHUNK):
        rows = slice(c * QKVU_CHUNK, (c + 1) * QKVU_CHUNK)
        hb = h_ref[rows, :].astype(bf16)
        q = (_dot(hb, w_ref[:, 0:D_ATTN]) + b_ref[:, 0:D_ATTN]) * QK_SCALE
        qlo_ref[rows, :] = jnp.where(even_head, q, 0.0).astype(bf16)
        qhi_ref[rows, :] = jnp.where(even_head, 0.0, q).astype(bf16)
        k = _dot(hb, w_ref[:, D_ATTN:2 * D_ATTN]) + b_ref[:, D_ATTN:2 * D_ATTN]
        k_ref[rows, :] = k.astype(bf16)
        v = _dot(hb, w_ref[:, 2 * D_ATTN:3 * D_ATTN]) + b_ref[:, 2 * D_ATTN:3 * D_ATTN]
        v_ref[rows, :] = v.astype(bf16)
        u = _dot(hb, w_ref[:, 3 * D_ATTN:3 * D_ATTN + D_POOL]) + b_ref[:, 3 * D_ATTN:3 * D_ATTN + D_POOL]
        upad_sc[POOL_HALO + c * QKVU_CHUNK:POOL_HALO + (c + 1) * QKVU_CHUNK, :] = u

    for c in range(SEQ // POOL_CHUNK):
        t = lax.broadcasted_iota(jnp.int32, (POOL_CHUNK, 1), 0) + c * POOL_CHUNK
        base = POOL_HALO + c * POOL_CHUNK
        for g, w in enumerate(POOL_WINDOWS):
            half = w // 2
            cols = slice(g * POOL_GROUP_DIM, (g + 1) * POOL_GROUP_DIM)
            acc = upad_sc[base - half:base - half + POOL_CHUNK, cols]
            for j in range(-half + 1, half):
                acc = acc + upad_sc[base + j:base + j + POOL_CHUNK, cols]
            cnt = (jnp.minimum(t + half, SEQ) - jnp.maximum(t - half, 0)).astype(f32)
            centre = upad_sc[base:base + POOL_CHUNK, cols]
            pd_sc[c * POOL_CHUNK:(c + 1) * POOL_CHUNK, cols] = (acc / cnt - centre).astype(bf16)

    for g in range(len(POOL_WINDOWS)):
        cols = slice(g * POOL_GROUP_DIM, (g + 1) * POOL_GROUP_DIM)
        pooled = _dot(pd_sc[:, cols], pw_ref[g]) * ps_ref[:, cols]
        pooled_ref[:, cols] = pooled.astype(bf16)


def _qkvu(layer, h3d, w_in, b_in, pool_w, pool_scale):
    n_qkvu = 3 * D_ATTN + D_POOL
    seq_spec = lambda width: pl.BlockSpec((None, SEQ, width), lambda b, l: (b, 0, 0))
    out_sds = jax.ShapeDtypeStruct((BATCH, SEQ, D_ATTN), bf16)
    grid_spec = pltpu.PrefetchScalarGridSpec(
        num_scalar_prefetch=1,
        grid=(BATCH,),
        in_specs=[
            seq_spec(D_MODEL),
            _HBM,
            _layer_block((1, n_qkvu)),
            _layer_block((len(POOL_WINDOWS), POOL_GROUP_DIM, POOL_GROUP_DIM)),
            _layer_block((1, D_POOL)),
        ],
        out_specs=[seq_spec(D_ATTN)] * 5,
        scratch_shapes=[
            pltpu.VMEM((SEQ + 2 * POOL_HALO, D_POOL), f32),
            pltpu.VMEM((SEQ, D_POOL), bf16),
            pltpu.VMEM((D_MODEL, n_qkvu), bf16),
            pltpu.VMEM((STAGE_SLOTS, STAGE_ROWS, n_qkvu), f32),
            pltpu.SemaphoreType.DMA((STAGE_SLOTS,)),
        ],
    )
    return pl.pallas_call(
        _qkvu_kernel,
        grid_spec=grid_spec,
        out_shape=[out_sds] * 5,
        compiler_params=pltpu.CompilerParams(
            dimension_semantics=("arbitrary",), vmem_limit_bytes=VMEM_LIMIT),
        name="qkvu",
    )(layer, h3d, w_in, b_in, pool_w, pool_scale)


def _mix_kernel(layer_ref, h_ref, qlo_ref, qhi_ref, k_ref, v_ref, pooled_ref, bias_ref, valid_ref,
                w_in_hbm, bg_ref, wa_hbm, wp_hbm, wm_hbm, g_ref, b_ref,
                o_ref, attn_sc, e_sc, l_sc, wg_ref, wa_ref, wp_ref, wm_ref, stage_sc, sem):
    c = pl.program_id(1)

    @pl.when((pl.program_id(0) == 0) & (c == 0))
    def _():
        layer = layer_ref[0]
        jobs = (_weight_jobs(w_in_hbm, layer, wg_ref, D_MODEL, 3 * D_ATTN + D_POOL, 2 * D_MODEL)
                + _weight_jobs(wa_hbm, layer, wa_ref, D_ATTN, 0, D_MODEL)
                + _weight_jobs(wp_hbm, layer, wp_ref, D_POOL, 0, D_MODEL)
                + _weight_jobs(wm_hbm, layer, wm_ref, D_MODEL, 0, D_MODEL))
        _load_weights_as_bf16(jobs, stage_sc, sem)

    even_head = lax.broadcasted_iota(jnp.int32, (GRID_W, LANES), 1) < HEAD_DIM

    def window(rr):
        r = c * MIX_ROWS + rr
        rs = jnp.clip(r - KH // 2, 0, ROWS - KH)
        return r - rs, pl.multiple_of(rs * GRID_W, GRID_W)

    def scores_softmax(rr, slot):
        delta, k0 = window(rr)
        rows = slice(rr * GRID_W, (rr + 1) * GRID_W)
        valid = valid_ref[...] > 0.5
        for p in range(N_PAIRS):
            cols = slice(p * LANES, (p + 1) * LANES)
            lhs = jnp.concatenate([qlo_ref[rows, cols], qhi_ref[rows, cols]], axis=0)
            kw = k_ref[pl.ds(k0, WIN_KEYS), cols]
            s = lax.dot_general(lhs, kw, (((1,), (1,)), ((), ())),
                                preferred_element_type=f32)
            parts = []
            for j in range(WIN_KEYS // LANES):
                sj = s[:, j * LANES:(j + 1) * LANES] + bias_ref[2 * j + KH - 1 - delta, p]
                parts.append(jnp.where(valid, sj, NEG_INF))
            m = functools.reduce(jnp.maximum, parts)
            m = jnp.max(m, axis=-1, keepdims=True)
            es = [jnp.exp(sj - m) for sj in parts]
            l_sc[slot, p] = jnp.sum(functools.reduce(lambda x, y: x + y, es), axis=-1, keepdims=True)
            for j, ej in enumerate(es):
                e_sc[slot, p, :, j * LANES:(j + 1) * LANES] = ej.astype(bf16)

    def weighted_values(rr, slot):
        _, k0 = window(rr)
        rows = slice(rr * GRID_W, (rr + 1) * GRID_W)
        for p in range(N_PAIRS):
            cols = slice(p * LANES, (p + 1) * LANES)
            vw = v_ref[pl.ds(k0, WIN_KEYS), cols]
            o = _dot(e_sc[slot, p], vw) / l_sc[slot, p]
            o = jnp.where(even_head, o[:GRID_W], o[GRID_W:])
            attn_sc[rows, cols] = o.astype(bf16)

    for it in range(MIX_ROWS + 1):
        if it < MIX_ROWS:
            scores_softmax(it, it % 2)
        if it >= 1:
            weighted_values(it - 1, (it - 1) % 2)

    part_rows = MIX_TILE // MIX_DENSE_PARTS
    for part in range(MIX_DENSE_PARTS):
        rows = slice(part * part_rows, (part + 1) * part_rows)
        h = h_ref[rows, :]
        hb = h.astype(bf16)
        ga = jax.nn.sigmoid(_dot(hb, wg_ref[:, 0:D_MODEL]) + bg_ref[:, 0:D_MODEL])
        merged = ga * _dot(attn_sc[rows, :], wa_ref[...])
        gb = jax.nn.sigmoid(_dot(hb, wg_ref[:, D_MODEL:2 * D_MODEL]) + bg_ref[:, D_MODEL:2 * D_MODEL])
        merged = merged + gb * _dot(pooled_ref[rows, :], wp_ref[...])
        z = ALPHA * h + _dot(merged.astype(bf16), wm_ref[...])
        o_ref[rows, :] = _layer_norm(z, g_ref[...], b_ref[...])


def _mix(layer, h3d, qlo, qhi, k, v, pooled, bias_tbl, valid_tbl,
         w_in, b_in, w_attn_out, w_pool_out, w_mix_out, ln_g, ln_b):
    tile = lambda width: pl.BlockSpec((None, MIX_TILE, width), lambda b, c, l: (b, c, 0))
    seq = lambda width: pl.BlockSpec((None, SEQ, width), lambda b, c, l: (b, 0, 0))
    grid_spec = pltpu.PrefetchScalarGridSpec(
        num_scalar_prefetch=1,
        grid=(BATCH, SEQ // MIX_TILE),
        in_specs=[
            tile(D_MODEL), tile(D_ATTN), tile(D_ATTN), seq(D_ATTN), seq(D_ATTN), tile(D_POOL),
            _layer_block((2 * KH - 2, N_PAIRS, 2 * GRID_W, LANES), single_buffer=True),
            pl.BlockSpec((2 * GRID_W, LANES), lambda b, c, l: (0, 0)),
            _HBM,
            _layer_block((1, 2 * D_MODEL), col_block=1),
            _HBM, _HBM, _HBM,
            _layer_block((1, D_MODEL)), _layer_block((1, D_MODEL)),
        ],
        out_specs=tile(D_MODEL),
        scratch_shapes=[
            pltpu.VMEM((MIX_TILE, D_ATTN), bf16),
            pltpu.VMEM((2, N_PAIRS, 2 * GRID_W, WIN_KEYS), bf16),
            pltpu.VMEM((2, N_PAIRS, 2 * GRID_W, 1), f32),
            pltpu.VMEM((D_MODEL, 2 * D_MODEL), bf16),
            pltpu.VMEM((D_ATTN, D_MODEL), bf16),
            pltpu.VMEM((D_POOL, D_MODEL), bf16),
            pltpu.VMEM((D_MODEL, D_MODEL), bf16),
            pltpu.VMEM((STAGE_SLOTS, STAGE_ROWS, 2 * D_MODEL), f32),
            pltpu.SemaphoreType.DMA((STAGE_SLOTS,)),
        ],
    )
    return pl.pallas_call(
        _mix_kernel,
        grid_spec=grid_spec,
        out_shape=jax.ShapeDtypeStruct((BATCH, SEQ, D_MODEL), f32),
        compiler_params=pltpu.CompilerParams(
            dimension_semantics=("arbitrary", "arbitrary"), vmem_limit_bytes=VMEM_LIMIT),
        name="mix",
    )(layer, h3d, qlo, qhi, k, v, pooled, bias_tbl, valid_tbl,
      w_in, b_in, w_attn_out, w_pool_out, w_mix_out, ln_g, ln_b)


def _ffn_kernel(layer_ref, h_ref, hprev_ref, hnext_ref, p_ref,
                w_up_hbm, cw_ref, cb_ref, wd_hbm, wpg_hbm, wpp_hbm, g_ref, b_ref,
                o_ref, hext_sc, val_sc, gate_sc, act_sc, acc_sc,
                wv_ref, wgt_ref, wd_ref, wpg_ref, wpp_ref, stage_sc, sem):
    i = pl.program_id(0)

    @pl.when(i == 0)
    def _():
        layer = layer_ref[0]
        jobs = (_weight_jobs(wpg_hbm, layer, wpg_ref, D_MODEL, 0, D_MODEL)
                + _weight_jobs(wpp_hbm, layer, wpp_ref, PLE_DIM, 0, D_MODEL)
                + _weight_jobs(w_up_hbm, layer, wv_ref, D_MODEL, 0, D_FF)
                + _weight_jobs(w_up_hbm, layer, wgt_ref, D_MODEL, D_FF, D_FF)
                + _weight_jobs(wd_hbm, layer, wd_ref, D_FF, 0, D_MODEL))
        _load_weights_as_bf16(jobs, stage_sc, sem)

    tiles_per_seq = SEQ // FFN_TILE
    pos = i % tiles_per_seq
    keep_prev = (pos != 0).astype(f32)
    keep_next = (pos != tiles_per_seq - 1).astype(f32)

    h = h_ref[...]
    hb = h.astype(bf16)
    main = slice(FFN_HALO, FFN_HALO + FFN_TILE)
    hext_sc[main, :] = hb
    hext_sc[0:FFN_HALO, :] = (hprev_ref[...] * keep_prev).astype(bf16)
    hext_sc[FFN_HALO + FFN_TILE:, :] = (hnext_ref[...] * keep_next).astype(bf16)

    ple = jax.nn.sigmoid(_dot(hb, wpg_ref[...])) * _dot(p_ref[...].astype(bf16), wpp_ref[...])
    acc_sc[...] = ALPHA * h + ple

    def up(c, slot):
        cols = slice(c * FF_CHUNK, (c + 1) * FF_CHUNK)
        val_sc[slot] = _dot(hext_sc[main, :], wv_ref[:, cols])
        gate_sc[slot] = _dot(hext_sc[...], wgt_ref[:, cols])

    def activate(c, slot):
        cols = slice(c * FF_CHUNK, (c + 1) * FF_CHUNK)
        gate = gate_sc[slot]
        n = FFN_TILE + 2 * FFN_HALO
        prev = pltpu.roll(gate, 1, 0)[main]
        nxt = pltpu.roll(gate, n - 1, 0)[main]
        y = prev * cw_ref[0:1, cols] + gate[main] * cw_ref[1:2, cols] + nxt * cw_ref[2:3, cols] + cb_ref[:, cols]
        act = 0.5 * y * (1.0 + lax.erf(y * np.float32(np.sqrt(0.5))))
        half = slice((c % 2) * FF_CHUNK, (c % 2 + 1) * FF_CHUNK)
        act_sc[(c // 2) % 2, :, half] = (act * val_sc[slot]).astype(bf16)

    def down(c0, n_chunks):
        k = n_chunks * FF_CHUNK
        acc_sc[...] += _dot(act_sc[(c0 // 2) % 2, :, 0:k], wd_ref[c0 * FF_CHUNK:c0 * FF_CHUNK + k, :])

    for it in range(N_FF_CHUNKS + 2):
        if it < N_FF_CHUNKS:
            up(it, it % 2)
        if 1 <= it <= N_FF_CHUNKS:
            activate(it - 1, (it - 1) % 2)
        c = it - 2
        if c >= 0 and c % 2 == 1:
            down(c - 1, 2)
        elif c == N_FF_CHUNKS - 1:
            down(c, 1)
    o_ref[...] = _layer_norm(acc_sc[...], g_ref[...], b_ref[...])


def _ffn(layer, h2d, p3d, w_up, conv_w, conv_b, w_down, w_ple_gate, w_ple_proj, ln_g, ln_b):
    halo_blocks = FFN_TILE // FFN_HALO
    n_halo_blocks = TOKENS // FFN_HALO
    grid_spec = pltpu.PrefetchScalarGridSpec(
        num_scalar_prefetch=1,
        grid=(TOKENS // FFN_TILE,),
        in_specs=[
            pl.BlockSpec((FFN_TILE, D_MODEL), lambda i, l: (i, 0)),
            pl.BlockSpec((FFN_HALO, D_MODEL), lambda i, l: (jnp.maximum(i * halo_blocks - 1, 0), 0)),
            pl.BlockSpec((FFN_HALO, D_MODEL),
                         lambda i, l: (jnp.minimum((i + 1) * halo_blocks, n_halo_blocks - 1), 0)),
            pl.BlockSpec((None, FFN_TILE, PLE_DIM), lambda i, l: (l[0], i, 0)),
            _HBM,
            _layer_block((3, D_FF)),
            _layer_block((1, D_FF)),
            _HBM, _HBM, _HBM,
            _layer_block((1, D_MODEL)), _layer_block((1, D_MODEL)),
        ],
        out_specs=pl.BlockSpec((FFN_TILE, D_MODEL), lambda i, l: (i, 0)),
        scratch_shapes=[
            pltpu.VMEM((FFN_TILE + 2 * FFN_HALO, D_MODEL), bf16),
            pltpu.VMEM((2, FFN_TILE, FF_CHUNK), f32),
            pltpu.VMEM((2, FFN_TILE + 2 * FFN_HALO, FF_CHUNK), f32),
            pltpu.VMEM((2, FFN_TILE, 2 * FF_CHUNK), bf16),
            pltpu.VMEM((FFN_TILE, D_MODEL), f32),
            pltpu.VMEM((D_MODEL, D_FF), bf16),
            pltpu.VMEM((D_MODEL, D_FF), bf16),
            pltpu.VMEM((D_FF, D_MODEL), bf16),
            pltpu.VMEM((D_MODEL, D_MODEL), bf16),
            pltpu.VMEM((PLE_DIM, D_MODEL), bf16),
            pltpu.VMEM((STAGE_SLOTS, STAGE_ROWS, D_FF), f32),
            pltpu.SemaphoreType.DMA((STAGE_SLOTS,)),
        ],
    )
    return pl.pallas_call(
        _ffn_kernel,
        grid_spec=grid_spec,
        out_shape=jax.ShapeDtypeStruct((TOKENS, D_MODEL), f32),
        compiler_params=pltpu.CompilerParams(
            dimension_semantics=("arbitrary",), vmem_limit_bytes=VMEM_LIMIT),
        name="ffn",
    )(layer, h2d, h2d, h2d, p3d, w_up, conv_w, conv_b, w_down, w_ple_gate, w_ple_proj, ln_g, ln_b)


def _column_geometry():
    qc = np.arange(GRID_W)[:, None]
    kc = np.arange(GRID_W)[None, :]
    col_start = np.clip(qc - KW // 2, 0, GRID_W - KW)
    valid = (kc >= col_start) & (kc < col_start + KW)
    col_off = np.clip(kc - qc, -(KW - 1), KW - 1) + KW - 1
    onehot = (col_off[..., None] == np.arange(2 * KW - 1)).astype(np.float32)
    valid_tbl = np.tile(valid.astype(np.float32), (2, 2))
    return onehot, valid_tbl


def _bias_tables(rpb):
    onehot, _ = _column_geometry()
    select = np.einsum('ij,qkc->qikjc', np.eye(2, dtype=np.float32), onehot)
    select = select.reshape(GRID_W, LANES, 2, 2 * KW - 1)
    two_rows = jnp.stack([rpb[:, :, :-1], rpb[:, :, 1:]], axis=3)
    tbl = jnp.einsum('lhrjc,qmjc->lrhqm', two_rows, jnp.asarray(select),
                     precision=lax.Precision.HIGHEST)
    return tbl.reshape(DEPTH, 2 * KH - 2, N_PAIRS, 2 * GRID_W, LANES)


def kernel(x, p, ln_in_g, ln_in_b, w_in, b_in, rpb, w_attn_out, pool_w, pool_scale, w_pool_out,
           w_mix_out, ln1_g, ln1_b, w_up, conv_w, conv_b, w_down, w_ple_gate, w_ple_proj,
           ln2_g, ln2_b):
    valid_tbl = jnp.asarray(_column_geometry()[1])
    bias_tbl = _bias_tables(rpb)
    p3d = p.reshape(DEPTH, TOKENS, PLE_DIM)
    row = lambda a: a.reshape(1, -1)
    rows = lambda a: a.reshape(DEPTH, 1, -1)
    pool_w_b = pool_w.astype(bf16)
    b_in_r, pool_scale_r, conv_b_r = rows(b_in), rows(pool_scale), rows(conv_b)
    ln1_g_r, ln1_b_r, ln2_g_r, ln2_b_r = rows(ln1_g), rows(ln1_b), rows(ln2_g), rows(ln2_b)

    h = _input_ln(x.reshape(TOKENS, D_MODEL), row(ln_in_g), row(ln_in_b))
    for i in range(DEPTH):
        layer = jnp.full((1,), i, jnp.int32)
        h3d = h.reshape(BATCH, SEQ, D_MODEL)
        qlo, qhi, k, v, pooled = _qkvu(layer, h3d, w_in, b_in_r, pool_w_b, pool_scale_r)
        h3d = _mix(layer, h3d, qlo, qhi, k, v, pooled, bias_tbl, valid_tbl,
                   w_in, b_in_r, w_attn_out, w_pool_out, w_mix_out, ln1_g_r, ln1_b_r)
        h = _ffn(layer, h3d.reshape(TOKENS, D_MODEL), p3d, w_up, conv_w, conv_b_r, w_down,
                 w_ple_gate, w_ple_proj, ln2_g_r, ln2_b_r)
    return h.reshape(BATCH, SEQ, D_MODEL)
```

```python
import functools

import numpy as np
import jax
import jax.numpy as jnp
from jax import lax
from jax.experimental import pallas as pl
from jax.experimental.pallas import tpu as pltpu

D_MODEL = 1024
BATCH = 8
SEQ = 2048
DEPTH = 4
GRID_W = 64
ROWS = SEQ // GRID_W
N_HEADS = 8
HEAD_DIM = 64
D_ATTN = N_HEADS * HEAD_DIM
N_PAIRS = N_HEADS // 2
KH = 8
KW = 16
WIN_KEYS = KH * GRID_W
POOL_WINDOWS = (2, 4, 8, 16)
D_POOL = 512
POOL_GROUP_DIM = 128
D_FF = 2816
FF_CHUNK = 256
N_FF_CHUNKS = D_FF // FF_CHUNK
PLE_DIM = 256
ALPHA = (2 * DEPTH) ** 0.25
LN_EPS = 1e-5
NEG_INF = -1e30
QK_SCALE = HEAD_DIM ** -0.5

LANES = 128
BF16_SUBLANES = 16
VMEM_LIMIT = 56 * 1024 * 1024

TOKENS = BATCH * SEQ
QKVU_CHUNK = 512
POOL_CHUNK = 256
POOL_HALO = 16
MIX_ROWS = 8
MIX_TILE = MIX_ROWS * GRID_W
MIX_DENSE_PARTS = 2
FFN_TILE = 512
FFN_HALO = BF16_SUBLANES
STAGE_ROWS = 256
STAGE_SLOTS = 3

f32 = jnp.float32
bf16 = jnp.bfloat16


def _layer_norm(xf, g, b):
    mu = jnp.mean(xf, axis=-1, keepdims=True)
    xc = xf - mu
    var = jnp.mean(xc * xc, axis=-1, keepdims=True)
    return xc * lax.rsqrt(var + LN_EPS) * g + b


def _dot(a, b):
    return jnp.dot(a, b, preferred_element_type=f32)


def _layer_block(shape, col_block=0, single_buffer=False):
    def index_map(*args):
        layer = args[-1][0]
        return (layer,) + (0,) * (len(shape) - 1) + (col_block,)
    mode = dict(pipeline_mode=pl.Buffered(1)) if single_buffer else {}
    return pl.BlockSpec((None,) + tuple(shape), index_map, **mode)


def _load_weights_as_bf16(jobs, stage_sc, sem):
    def copy(j):
        src, _, _, rows, cols = jobs[j]
        slot = j % STAGE_SLOTS
        return pltpu.make_async_copy(src, stage_sc.at[slot, pl.ds(0, rows), pl.ds(0, cols)], sem.at[slot])

    for j in range(min(STAGE_SLOTS - 1, len(jobs))):
        copy(j).start()
    for j, (_, dst, row0, rows, cols) in enumerate(jobs):
        if j + STAGE_SLOTS - 1 < len(jobs):
            copy(j + STAGE_SLOTS - 1).start()
        copy(j).wait()
        dst[row0:row0 + rows, :] = stage_sc[j % STAGE_SLOTS, 0:rows, 0:cols].astype(bf16)


def _weight_jobs(w_hbm, layer, dst, n_rows, col0, cols):
    return [(w_hbm.at[layer, pl.ds(r0, min(STAGE_ROWS, n_rows - r0)), pl.ds(col0, cols)],
             dst, r0, min(STAGE_ROWS, n_rows - r0), cols)
            for r0 in range(0, n_rows, STAGE_ROWS)]


_HBM = pl.BlockSpec(memory_space=pl.ANY)


def _qkvu_kernel(layer_ref, h_ref, *refs, input_norm):
    norm_refs, refs = (refs[:2], refs[2:]) if input_norm else ((), refs)
    (w_hbm, b_ref, pw_ref, ps_ref, qlo_ref, qhi_ref, k_ref, v_ref, pooled_ref,
     upad_sc, pd_sc, w_ref, stage_sc, sem) = refs
    n_qkvu = 3 * D_ATTN + D_POOL

    @pl.when(pl.program_id(0) == 0)
    def _():
        _load_weights_as_bf16(_weight_jobs(w_hbm, layer_ref[0], w_ref, D_MODEL, 0, n_qkvu), stage_sc, sem)

    lane = lax.broadcasted_iota(jnp.int32, (QKVU_CHUNK, D_ATTN), 1)
    even_head = (lane % LANES) < HEAD_DIM

    zeros_halo = jnp.zeros((POOL_HALO, D_POOL), f32)
    upad_sc[0:POOL_HALO, :] = zeros_halo
    upad_sc[POOL_HALO + SEQ:POOL_HALO + SEQ + POOL_HALO, :] = zeros_halo

    for c in range(SEQ // QKVU_CHUNK):
        rows = slice(c * QKVU_CHUNK, (c + 1) * QKVU_CHUNK)
        h = h_ref[rows, :]
        if input_norm:
            h = _layer_norm(h, norm_refs[0][...], norm_refs[1][...])
        hb = h.astype(bf16)
        q = (_dot(hb, w_ref[:, 0:D_ATTN]) + b_ref[:, 0:D_ATTN]) * QK_SCALE
        qlo_ref[rows, :] = jnp.where(even_head, q, 0.0).astype(bf16)
        qhi_ref[rows, :] = jnp.where(even_head, 0.0, q).astype(bf16)
        k = _dot(hb, w_ref[:, D_ATTN:2 * D_ATTN]) + b_ref[:, D_ATTN:2 * D_ATTN]
        k_ref[rows, :] = k.astype(bf16)
        v = _dot(hb, w_ref[:, 2 * D_ATTN:3 * D_ATTN]) + b_ref[:, 2 * D_ATTN:3 * D_ATTN]
        v_ref[rows, :] = v.astype(bf16)
        u = _dot(hb, w_ref[:, 3 * D_ATTN:3 * D_ATTN + D_POOL]) + b_ref[:, 3 * D_ATTN:3 * D_ATTN + D_POOL]
        upad_sc[POOL_HALO + c * QKVU_CHUNK:POOL_HALO + (c + 1) * QKVU_CHUNK, :] = u

    for c in range(SEQ // POOL_CHUNK):
        t = lax.broadcasted_iota(jnp.int32, (POOL_CHUNK, 1), 0) + c * POOL_CHUNK
        base = POOL_HALO + c * POOL_CHUNK
        for g, w in enumerate(POOL_WINDOWS):
            half = w // 2
            cols = slice(g * POOL_GROUP_DIM, (g + 1) * POOL_GROUP_DIM)
            acc = upad_sc[base - half:base - half + POOL_CHUNK, cols]
            for j in range(-half + 1, half):
                acc = acc + upad_sc[base + j:base + j + POOL_CHUNK, cols]
            cnt = (jnp.minimum(t + half, SEQ) - jnp.maximum(t - half, 0)).astype(f32)
            centre = upad_sc[base:base + POOL_CHUNK, cols]
            pd_sc[c * POOL_CHUNK:(c + 1) * POOL_CHUNK, cols] = (acc / cnt - centre).astype(bf16)

    for g in range(len(POOL_WINDOWS)):
        cols = slice(g * POOL_GROUP_DIM, (g + 1) * POOL_GROUP_DIM)
        pooled = _dot(pd_sc[:, cols], pw_ref[g]) * ps_ref[:, cols]
        pooled_ref[:, cols] = pooled.astype(bf16)


def _qkvu(layer, h3d, input_norm, w_in, b_in, pool_w, pool_scale):
    n_qkvu = 3 * D_ATTN + D_POOL
    seq_spec = lambda width: pl.BlockSpec((None, SEQ, width), lambda b, l: (b, 0, 0))
    norm_specs = [pl.BlockSpec((1, D_MODEL), lambda b, l: (0, 0))] * 2 if input_norm else []
    out_sds = jax.ShapeDtypeStruct((BATCH, SEQ, D_ATTN), bf16)
    grid_spec = pltpu.PrefetchScalarGridSpec(
        num_scalar_prefetch=1,
        grid=(BATCH,),
        in_specs=[
            seq_spec(D_MODEL),
            *norm_specs,
            _HBM,
            _layer_block((1, n_qkvu)),
            _layer_block((len(POOL_WINDOWS), POOL_GROUP_DIM, POOL_GROUP_DIM)),
            _layer_block((1, D_POOL)),
        ],
        out_specs=[seq_spec(D_ATTN)] * 5,
        scratch_shapes=[
            pltpu.VMEM((SEQ + 2 * POOL_HALO, D_POOL), f32),
            pltpu.VMEM((SEQ, D_POOL), bf16),
            pltpu.VMEM((D_MODEL, n_qkvu), bf16),
            pltpu.VMEM((STAGE_SLOTS, STAGE_ROWS, n_qkvu), f32),
            pltpu.SemaphoreType.DMA((STAGE_SLOTS,)),
        ],
    )
    return pl.pallas_call(
        functools.partial(_qkvu_kernel, input_norm=bool(input_norm)),
        grid_spec=grid_spec,
        out_shape=[out_sds] * 5,
        compiler_params=pltpu.CompilerParams(
            dimension_semantics=("arbitrary",), vmem_limit_bytes=VMEM_LIMIT),
        name="qkvu",
    )(layer, h3d, *(input_norm or ()), w_in, b_in, pool_w, pool_scale)


def _mix_kernel(layer_ref, h_ref, *refs, input_norm):
    norm_refs, refs = (refs[:2], refs[2:]) if input_norm else ((), refs)
    (qlo_ref, qhi_ref, k_ref, v_ref, pooled_ref, bias_ref, valid_ref,
     w_in_hbm, bg_ref, wa_hbm, wp_hbm, wm_hbm, g_ref, b_ref,
     o_ref, attn_sc, e_sc, l_sc, wg_ref, wa_ref, wp_ref, wm_ref, stage_sc, sem) = refs
    c = pl.program_id(1)

    @pl.when((pl.program_id(0) == 0) & (c == 0))
    def _():
        layer = layer_ref[0]
        jobs = (_weight_jobs(w_in_hbm, layer, wg_ref, D_MODEL, 3 * D_ATTN + D_POOL, 2 * D_MODEL)
                + _weight_jobs(wa_hbm, layer, wa_ref, D_ATTN, 0, D_MODEL)
                + _weight_jobs(wp_hbm, layer, wp_ref, D_POOL, 0, D_MODEL)
                + _weight_jobs(wm_hbm, layer, wm_ref, D_MODEL, 0, D_MODEL))
        _load_weights_as_bf16(jobs, stage_sc, sem)

    even_head = lax.broadcasted_iota(jnp.int32, (GRID_W, LANES), 1) < HEAD_DIM

    def window(rr):
        r = c * MIX_ROWS + rr
        rs = jnp.clip(r - KH // 2, 0, ROWS - KH)
        return r - rs, pl.multiple_of(rs * GRID_W, GRID_W)

    def scores_softmax(rr, slot):
        delta, k0 = window(rr)
        rows = slice(rr * GRID_W, (rr + 1) * GRID_W)
        valid = valid_ref[...] > 0.5
        for p in range(N_PAIRS):
            cols = slice(p * LANES, (p + 1) * LANES)
            lhs = jnp.concatenate([qlo_ref[rows, cols], qhi_ref[rows, cols]], axis=0)
            kw = k_ref[pl.ds(k0, WIN_KEYS), cols]
            s = lax.dot_general(lhs, kw, (((1,), (1,)), ((), ())),
                                preferred_element_type=f32)
            parts = []
            for j in range(WIN_KEYS // LANES):
                sj = s[:, j * LANES:(j + 1) * LANES] + bias_ref[2 * j + KH - 1 - delta, p]
                parts.append(jnp.where(valid, sj, NEG_INF))
            m = functools.reduce(jnp.maximum, parts)
            m = jnp.max(m, axis=-1, keepdims=True)
            es = [jnp.exp(sj - m) for sj in parts]
            l_sc[slot, p] = jnp.sum(functools.reduce(lambda x, y: x + y, es), axis=-1, keepdims=True)
            for j, ej in enumerate(es):
                e_sc[slot, p, :, j * LANES:(j + 1) * LANES] = ej.astype(bf16)

    def weighted_values(rr, slot):
        _, k0 = window(rr)
        rows = slice(rr * GRID_W, (rr + 1) * GRID_W)
        for p in range(N_PAIRS):
            cols = slice(p * LANES, (p + 1) * LANES)
            vw = v_ref[pl.ds(k0, WIN_KEYS), cols]
            o = _dot(e_sc[slot, p], vw) / l_sc[slot, p]
            o = jnp.where(even_head, o[:GRID_W], o[GRID_W:])
            attn_sc[rows, cols] = o.astype(bf16)

    for it in range(MIX_ROWS + 1):
        if it < MIX_ROWS:
            scores_softmax(it, it % 2)
        if it >= 1:
            weighted_values(it - 1, (it - 1) % 2)

    part_rows = MIX_TILE // MIX_DENSE_PARTS
    for part in range(MIX_DENSE_PARTS):
        rows = slice(part * part_rows, (part + 1) * part_rows)
        h = h_ref[rows, :]
        if input_norm:
            h = _layer_norm(h, norm_refs[0][...], norm_refs[1][...])
        hb = h.astype(bf16)
        ga = jax.nn.sigmoid(_dot(hb, wg_ref[:, 0:D_MODEL]) + bg_ref[:, 0:D_MODEL])
        merged = ga * _dot(attn_sc[rows, :], wa_ref[...])
        gb = jax.nn.sigmoid(_dot(hb, wg_ref[:, D_MODEL:2 * D_MODEL]) + bg_ref[:, D_MODEL:2 * D_MODEL])
        merged = merged + gb * _dot(pooled_ref[rows, :], wp_ref[...])
        z = ALPHA * h + _dot(merged.astype(bf16), wm_ref[...])
        o_ref[rows, :] = _layer_norm(z, g_ref[...], b_ref[...])


def _mix(layer, h3d, input_norm, qlo, qhi, k, v, pooled, bias_tbl, valid_tbl,
         w_in, b_in, w_attn_out, w_pool_out, w_mix_out, ln_g, ln_b):
    tile = lambda width: pl.BlockSpec((None, MIX_TILE, width), lambda b, c, l: (b, c, 0))
    seq = lambda width: pl.BlockSpec((None, SEQ, width), lambda b, c, l: (b, 0, 0))
    norm_specs = [pl.BlockSpec((1, D_MODEL), lambda b, c, l: (0, 0))] * 2 if input_norm else []
    grid_spec = pltpu.PrefetchScalarGridSpec(
        num_scalar_prefetch=1,
        grid=(BATCH, SEQ // MIX_TILE),
        in_specs=[
            tile(D_MODEL), *norm_specs,
            tile(D_ATTN), tile(D_ATTN), seq(D_ATTN), seq(D_ATTN), tile(D_POOL),
            _layer_block((2 * KH - 2, N_PAIRS, 2 * GRID_W, LANES), single_buffer=True),
            pl.BlockSpec((2 * GRID_W, LANES), lambda b, c, l: (0, 0)),
            _HBM,
            _layer_block((1, 2 * D_MODEL), col_block=1),
            _HBM, _HBM, _HBM,
            _layer_block((1, D_MODEL)), _layer_block((1, D_MODEL)),
        ],
        out_specs=tile(D_MODEL),
        scratch_shapes=[
            pltpu.VMEM((MIX_TILE, D_ATTN), bf16),
            pltpu.VMEM((2, N_PAIRS, 2 * GRID_W, WIN_KEYS), bf16),
            pltpu.VMEM((2, N_PAIRS, 2 * GRID_W, 1), f32),
            pltpu.VMEM((D_MODEL, 2 * D_MODEL), bf16),
            pltpu.VMEM((D_ATTN, D_MODEL), bf16),
            pltpu.VMEM((D_POOL, D_MODEL), bf16),
            pltpu.VMEM((D_MODEL, D_MODEL), bf16),
            pltpu.VMEM((STAGE_SLOTS, STAGE_ROWS, 2 * D_MODEL), f32),
            pltpu.SemaphoreType.DMA((STAGE_SLOTS,)),
        ],
    )
    return pl.pallas_call(
        functools.partial(_mix_kernel, input_norm=bool(input_norm)),
        grid_spec=grid_spec,
        out_shape=jax.ShapeDtypeStruct((BATCH, SEQ, D_MODEL), f32),
        compiler_params=pltpu.CompilerParams(
            dimension_semantics=("arbitrary", "arbitrary"), vmem_limit_bytes=VMEM_LIMIT),
        name="mix",
    )(layer, h3d, *(input_norm or ()), qlo, qhi, k, v, pooled, bias_tbl, valid_tbl,
      w_in, b_in, w_attn_out, w_pool_out, w_mix_out, ln_g, ln_b)


def _ffn_kernel(layer_ref, h_ref, hprev_ref, hnext_ref, p_ref,
                w_up_hbm, cw_ref, cb_ref, wd_hbm, wpg_hbm, wpp_hbm, g_ref, b_ref,
                o_ref, hext_sc, val_sc, gate_sc, act_sc, acc_sc,
                wv_ref, wgt_ref, wd_ref, wpg_ref, wpp_ref, stage_sc, sem):
    i = pl.program_id(0)

    @pl.when(i == 0)
    def _():
        layer = layer_ref[0]
        jobs = (_weight_jobs(wpg_hbm, layer, wpg_ref, D_MODEL, 0, D_MODEL)
                + _weight_jobs(wpp_hbm, layer, wpp_ref, PLE_DIM, 0, D_MODEL)
                + _weight_jobs(w_up_hbm, layer, wv_ref, D_MODEL, 0, D_FF)
                + _weight_jobs(w_up_hbm, layer, wgt_ref, D_MODEL, D_FF, D_FF)
                + _weight_jobs(wd_hbm, layer, wd_ref, D_FF, 0, D_MODEL))
        _load_weights_as_bf16(jobs, stage_sc, sem)

    tiles_per_seq = SEQ // FFN_TILE
    pos = i % tiles_per_seq
    keep_prev = (pos != 0).astype(f32)
    keep_next = (pos != tiles_per_seq - 1).astype(f32)

    h = h_ref[...]
    hb = h.astype(bf16)
    main = slice(FFN_HALO, FFN_HALO + FFN_TILE)
    hext_sc[main, :] = hb
    hext_sc[0:FFN_HALO, :] = (hprev_ref[...] * keep_prev).astype(bf16)
    hext_sc[FFN_HALO + FFN_TILE:, :] = (hnext_ref[...] * keep_next).astype(bf16)

    ple = jax.nn.sigmoid(_dot(hb, wpg_ref[...])) * _dot(p_ref[...].astype(bf16), wpp_ref[...])
    acc_sc[...] = ALPHA * h + ple

    def up(c, slot):
        cols = slice(c * FF_CHUNK, (c + 1) * FF_CHUNK)
        val_sc[slot] = _dot(hext_sc[main, :], wv_ref[:, cols])
        gate_sc[slot] = _dot(hext_sc[...], wgt_ref[:, cols])

    def activate(c, slot):
        cols = slice(c * FF_CHUNK, (c + 1) * FF_CHUNK)
        gate = gate_sc[slot]
        n = FFN_TILE + 2 * FFN_HALO
        prev = pltpu.roll(gate, 1, 0)[main]
        nxt = pltpu.roll(gate, n - 1, 0)[main]
        y = prev * cw_ref[0:1, cols] + gate[main] * cw_ref[1:2, cols] + nxt * cw_ref[2:3, cols] + cb_ref[:, cols]
        act = 0.5 * y * (1.0 + lax.erf(y * np.float32(np.sqrt(0.5))))
        half = slice((c % 2) * FF_CHUNK, (c % 2 + 1) * FF_CHUNK)
        act_sc[(c // 2) % 2, :, half] = (act * val_sc[slot]).astype(bf16)

    def down(c0, n_chunks):
        k = n_chunks * FF_CHUNK
        acc_sc[...] += _dot(act_sc[(c0 // 2) % 2, :, 0:k], wd_ref[c0 * FF_CHUNK:c0 * FF_CHUNK + k, :])

    for it in range(N_FF_CHUNKS + 2):
        if it < N_FF_CHUNKS:
            up(it, it % 2)
        if 1 <= it <= N_FF_CHUNKS:
            activate(it - 1, (it - 1) % 2)
        c = it - 2
        if c >= 0 and c % 2 == 1:
            down(c - 1, 2)
        elif c == N_FF_CHUNKS - 1:
            down(c, 1)
    o_ref[...] = _layer_norm(acc_sc[...], g_ref[...], b_ref[...])


def _ffn(layer, h2d, p3d, w_up, conv_w, conv_b, w_down, w_ple_gate, w_ple_proj, ln_g, ln_b):
    halo_blocks = FFN_TILE // FFN_HALO
    n_halo_blocks = TOKENS // FFN_HALO
    grid_spec = pltpu.PrefetchScalarGridSpec(
        num_scalar_prefetch=1,
        grid=(TOKENS // FFN_TILE,),
        in_specs=[
            pl.BlockSpec((FFN_TILE, D_MODEL), lambda i, l: (i, 0)),
            pl.BlockSpec((FFN_HALO, D_MODEL), lambda i, l: (jnp.maximum(i * halo_blocks - 1, 0), 0)),
            pl.BlockSpec((FFN_HALO, D_MODEL),
                         lambda i, l: (jnp.minimum((i + 1) * halo_blocks, n_halo_blocks - 1), 0)),
            pl.BlockSpec((None, FFN_TILE, PLE_DIM), lambda i, l: (l[0], i, 0)),
            _HBM,
            _layer_block((3, D_FF)),
            _layer_block((1, D_FF)),
            _HBM, _HBM, _HBM,
            _layer_block((1, D_MODEL)), _layer_block((1, D_MODEL)),
        ],
        out_specs=pl.BlockSpec((FFN_TILE, D_MODEL), lambda i, l: (i, 0)),
        scratch_shapes=[
            pltpu.VMEM((FFN_TILE + 2 * FFN_HALO, D_MODEL), bf16),
            pltpu.VMEM((2, FFN_TILE, FF_CHUNK), f32),
            pltpu.VMEM((2, FFN_TILE + 2 * FFN_HALO, FF_CHUNK), f32),
            pltpu.VMEM((2, FFN_TILE, 2 * FF_CHUNK), bf16),
            pltpu.VMEM((FFN_TILE, D_MODEL), f32),
            pltpu.VMEM((D_MODEL, D_FF), bf16),
            pltpu.VMEM((D_MODEL, D_FF), bf16),
            pltpu.VMEM((D_FF, D_MODEL), bf16),
            pltpu.VMEM((D_MODEL, D_MODEL), bf16),
            pltpu.VMEM((PLE_DIM, D_MODEL), bf16),
            pltpu.VMEM((STAGE_SLOTS, STAGE_ROWS, D_FF), f32),
            pltpu.SemaphoreType.DMA((STAGE_SLOTS,)),
        ],
    )
    return pl.pallas_call(
        _ffn_kernel,
        grid_spec=grid_spec,
        out_shape=jax.ShapeDtypeStruct((TOKENS, D_MODEL), f32),
        compiler_params=pltpu.CompilerParams(
            dimension_semantics=("arbitrary",), vmem_limit_bytes=VMEM_LIMIT),
        name="ffn",
    )(layer, h2d, h2d, h2d, p3d, w_up, conv_w, conv_b, w_down, w_ple_gate, w_ple_proj, ln_g, ln_b)


def _column_geometry():
    qc = np.arange(GRID_W)[:, None]
    kc = np.arange(GRID_W)[None, :]
    col_start = np.clip(qc - KW // 2, 0, GRID_W - KW)
    valid = (kc >= col_start) & (kc < col_start + KW)
    col_off = np.clip(kc - qc, -(KW - 1), KW - 1) + KW - 1
    onehot = (col_off[..., None] == np.arange(2 * KW - 1)).astype(np.float32)
    valid_tbl = np.tile(valid.astype(np.float32), (2, 2))
    return onehot, valid_tbl


def _bias_tables(rpb):
    onehot, _ = _column_geometry()
    select = np.einsum('ij,qkc->qikjc', np.eye(2, dtype=np.float32), onehot)
    select = select.reshape(GRID_W, LANES, 2, 2 * KW - 1)
    two_rows = jnp.stack([rpb[:, :, :-1], rpb[:, :, 1:]], axis=3)
    tbl = jnp.einsum('lhrjc,qmjc->lrhqm', two_rows, jnp.asarray(select),
                     precision=lax.Precision.HIGHEST)
    return tbl.reshape(DEPTH, 2 * KH - 2, N_PAIRS, 2 * GRID_W, LANES)


def kernel(x, p, ln_in_g, ln_in_b, w_in, b_in, rpb, w_attn_out, pool_w, pool_scale, w_pool_out,
           w_mix_out, ln1_g, ln1_b, w_up, conv_w, conv_b, w_down, w_ple_gate, w_ple_proj,
           ln2_g, ln2_b):
    valid_tbl = jnp.asarray(_column_geometry()[1])
    bias_tbl = _bias_tables(rpb)
    p3d = p.reshape(DEPTH, TOKENS, PLE_DIM)
    row = lambda a: a.reshape(1, -1)
    rows = lambda a: a.reshape(DEPTH, 1, -1)
    pool_w_b = pool_w.astype(bf16)
    b_in_r, pool_scale_r, conv_b_r = rows(b_in), rows(pool_scale), rows(conv_b)
    ln1_g_r, ln1_b_r, ln2_g_r, ln2_b_r = rows(ln1_g), rows(ln1_b), rows(ln2_g), rows(ln2_b)

    h3d = x
    for i in range(DEPTH):
        layer = jnp.full((1,), i, jnp.int32)
        input_norm = (row(ln_in_g), row(ln_in_b)) if i == 0 else None
        qlo, qhi, k, v, pooled = _qkvu(layer, h3d, input_norm, w_in, b_in_r, pool_w_b, pool_scale_r)
        h3d = _mix(layer, h3d, input_norm, qlo, qhi, k, v, pooled, bias_tbl, valid_tbl,
                   w_in, b_in_r, w_attn_out, w_pool_out, w_mix_out, ln1_g_r, ln1_b_r)
        h = _ffn(layer, h3d.reshape(TOKENS, D_MODEL), p3d, w_up, conv_w, conv_b_r, w_down,
                 w_ple_gate, w_ple_proj, ln2_g_r, ln2_b_r)
        h3d = h.reshape(BATCH, SEQ, D_MODEL)
    return h3d
```

```python
import functools

import numpy as np
import jax
import jax.numpy as jnp
from jax import lax
from jax.experimental import pallas as pl
from jax.experimental.pallas import tpu as pltpu

D_MODEL = 1024
BATCH = 8
SEQ = 2048
DEPTH = 4
GRID_W = 64
ROWS = SEQ // GRID_W
N_HEADS = 8
HEAD_DIM = 64
D_ATTN = N_HEADS * HEAD_DIM
N_PAIRS = N_HEADS // 2
KH = 8
KW = 16
WIN_KEYS = KH * GRID_W
POOL_WINDOWS = (2, 4, 8, 16)
D_POOL = 512
POOL_GROUP_DIM = 128
D_FF = 2816
FF_CHUNK = 256
N_FF_CHUNKS = D_FF // FF_CHUNK
PLE_DIM = 256
ALPHA = (2 * DEPTH) ** 0.25
LN_EPS = 1e-5
NEG_INF = -1e30
QK_SCALE = HEAD_DIM ** -0.5

LANES = 128
BF16_SUBLANES = 16
VMEM_LIMIT = 56 * 1024 * 1024

TOKENS = BATCH * SEQ
LN_TILE = 1024
QKVU_CHUNK = 512
POOL_CHUNK = 256
POOL_HALO = 16
MIX_ROWS = 8
MIX_TILE = MIX_ROWS * GRID_W
MIX_DENSE_PARTS = 2
FFN_TILE = 512
FFN_HALO = BF16_SUBLANES
STAGE_ROWS = 256
STAGE_SLOTS = 3

f32 = jnp.float32
bf16 = jnp.bfloat16


def _layer_norm(xf, g, b):
    mu = jnp.mean(xf, axis=-1, keepdims=True)
    xc = xf - mu
    var = jnp.mean(xc * xc, axis=-1, keepdims=True)
    return xc * lax.rsqrt(var + LN_EPS) * g + b


def _dot(a, b):
    return jnp.dot(a, b, preferred_element_type=f32)


def _layer_block(shape, col_block=0, single_buffer=False):
    def index_map(*args):
        layer = args[-1][0]
        return (layer,) + (0,) * (len(shape) - 1) + (col_block,)
    mode = dict(pipeline_mode=pl.Buffered(1)) if single_buffer else {}
    return pl.BlockSpec((None,) + tuple(shape), index_map, **mode)


def _load_weights_as_bf16(jobs, stage_sc, sem):
    def copy(j):
        src, _, _, rows, cols = jobs[j]
        slot = j % STAGE_SLOTS
        return pltpu.make_async_copy(src, stage_sc.at[slot, pl.ds(0, rows), pl.ds(0, cols)], sem.at[slot])

    for j in range(min(STAGE_SLOTS - 1, len(jobs))):
        copy(j).start()
    for j, (_, dst, row0, rows, cols) in enumerate(jobs):
        if j + STAGE_SLOTS - 1 < len(jobs):
            copy(j + STAGE_SLOTS - 1).start()
        copy(j).wait()
        dst[row0:row0 + rows, :] = stage_sc[j % STAGE_SLOTS, 0:rows, 0:cols].astype(bf16)


def _weight_jobs(w_hbm, layer, dst, n_rows, col0, cols):
    return [(w_hbm.at[layer, pl.ds(r0, min(STAGE_ROWS, n_rows - r0)), pl.ds(col0, cols)],
             dst, r0, min(STAGE_ROWS, n_rows - r0), cols)
            for r0 in range(0, n_rows, STAGE_ROWS)]


_HBM = pl.BlockSpec(memory_space=pl.ANY)


def _ln_kernel(x_ref, g_ref, b_ref, o_ref):
    o_ref[...] = _layer_norm(x_ref[...], g_ref[...], b_ref[...])


def _input_ln(x2d, g, b):
    return pl.pallas_call(
        _ln_kernel,
        grid=(TOKENS // LN_TILE,),
        in_specs=[
            pl.BlockSpec((LN_TILE, D_MODEL), lambda i: (i, 0)),
            pl.BlockSpec((1, D_MODEL), lambda i: (0, 0)),
            pl.BlockSpec((1, D_MODEL), lambda i: (0, 0)),
        ],
        out_specs=pl.BlockSpec((LN_TILE, D_MODEL), lambda i: (i, 0)),
        out_shape=jax.ShapeDtypeStruct((TOKENS, D_MODEL), f32),
        compiler_params=pltpu.CompilerParams(
            dimension_semantics=("arbitrary",), vmem_limit_bytes=VMEM_LIMIT),
        name="input_ln",
    )(x2d, g, b)


def _qkvu_kernel(layer_ref, h_ref, w_hbm, b_ref, pw_ref, ps_ref,
                 qlo_ref, qhi_ref, k_ref, v_ref, pooled_ref,
                 upad_sc, pd_sc, w_ref, stage_sc, sem):
    n_qkvu = 3 * D_ATTN + D_POOL

    @pl.when(pl.program_id(0) == 0)
    def _():
        _load_weights_as_bf16(_weight_jobs(w_hbm, layer_ref[0], w_ref, D_MODEL, 0, n_qkvu), stage_sc, sem)

    lane = lax.broadcasted_iota(jnp.int32, (QKVU_CHUNK, D_ATTN), 1)
    even_head = (lane % LANES) < HEAD_DIM

    zeros_halo = jnp.zeros((POOL_HALO, D_POOL), f32)
    upad_sc[0:POOL_HALO, :] = zeros_halo
    upad_sc[POOL_HALO + SEQ:POOL_HALO + SEQ + POOL_HALO, :] = zeros_halo

    for c in range(SEQ // QKVU_CHUNK):
        rows = slice(c * QKVU_CHUNK, (c + 1) * QKVU_CHUNK)
        hb = h_ref[rows, :].astype(bf16)
        q = (_dot(hb, w_ref[:, 0:D_ATTN]) + b_ref[:, 0:D_ATTN]) * QK_SCALE
        qlo_ref[rows, :] = jnp.where(even_head, q, 0.0).astype(bf16)
        qhi_ref[rows, :] = jnp.where(even_head, 0.0, q).astype(bf16)
        k = _dot(hb, w_ref[:, D_ATTN:2 * D_ATTN]) + b_ref[:, D_ATTN:2 * D_ATTN]
        k_ref[rows, :] = k.astype(bf16)
        v = _dot(hb, w_ref[:, 2 * D_ATTN:3 * D_ATTN]) + b_ref[:, 2 * D_ATTN:3 * D_ATTN]
        v_ref[rows, :] = v.astype(bf16)
        u = _dot(hb, w_ref[:, 3 * D_ATTN:3 * D_ATTN + D_POOL]) + b_ref[:, 3 * D_ATTN:3 * D_ATTN + D_POOL]
        upad_sc[POOL_HALO + c * QKVU_CHUNK:POOL_HALO + (c + 1) * QKVU_CHUNK, :] = u

    for c in range(SEQ // POOL_CHUNK):
        t = lax.broadcasted_iota(jnp.int32, (POOL_CHUNK, 1), 0) + c * POOL_CHUNK
        base = POOL_HALO + c * POOL_CHUNK
        for g, w in enumerate(POOL_WINDOWS):
            half = w // 2
            cols = slice(g * POOL_GROUP_DIM, (g + 1) * POOL_GROUP_DIM)
            acc = upad_sc[base - half:base - half + POOL_CHUNK, cols]
            for j in range(-half + 1, half):
                acc = acc + upad_sc[base + j:base + j + POOL_CHUNK, cols]
            cnt = (jnp.minimum(t + half, SEQ) - jnp.maximum(t - half, 0)).astype(f32)
            centre = upad_sc[base:base + POOL_CHUNK, cols]
            pd_sc[c * POOL_CHUNK:(c + 1) * POOL_CHUNK, cols] = (acc / cnt - centre).astype(bf16)

    for g in range(len(POOL_WINDOWS)):
        cols = slice(g * POOL_GROUP_DIM, (g + 1) * POOL_GROUP_DIM)
        pooled = _dot(pd_sc[:, cols], pw_ref[g]) * ps_ref[:, cols]
        pooled_ref[:, cols] = pooled.astype(bf16)


def _qkvu(layer, h3d, w_in, b_in, pool_w, pool_scale):
    n_qkvu = 3 * D_ATTN + D_POOL
    seq_spec = lambda width: pl.BlockSpec((None, SEQ, width), lambda b, l: (b, 0, 0))
    out_sds = jax.ShapeDtypeStruct((BATCH, SEQ, D_ATTN), bf16)
    grid_spec = pltpu.PrefetchScalarGridSpec(
        num_scalar_prefetch=1,
        grid=(BATCH,),
        in_specs=[
            seq_spec(D_MODEL),
            _HBM,
            _layer_block((1, n_qkvu)),
            _layer_block((len(POOL_WINDOWS), POOL_GROUP_DIM, POOL_GROUP_DIM)),
            _layer_block((1, D_POOL)),
        ],
        out_specs=[seq_spec(D_ATTN)] * 5,
        scratch_shapes=[
            pltpu.VMEM((SEQ + 2 * POOL_HALO, D_POOL), f32),
            pltpu.VMEM((SEQ, D_POOL), bf16),
            pltpu.VMEM((D_MODEL, n_qkvu), bf16),
            pltpu.VMEM((STAGE_SLOTS, STAGE_ROWS, n_qkvu), f32),
            pltpu.SemaphoreType.DMA((STAGE_SLOTS,)),
        ],
    )
    return pl.pallas_call(
        _qkvu_kernel,
        grid_spec=grid_spec,
        out_shape=[out_sds] * 5,
        compiler_params=pltpu.CompilerParams(
            dimension_semantics=("arbitrary",), vmem_limit_bytes=VMEM_LIMIT),
        name="qkvu",
    )(layer, h3d, w_in, b_in, pool_w, pool_scale)


def _mix_kernel(layer_ref, h_ref, qlo_ref, qhi_ref, k_ref, v_ref, pooled_ref, bias_ref, valid_ref,
                w_in_hbm, bg_ref, wa_hbm, wp_hbm, wm_hbm, g_ref, b_ref,
                o_ref, attn_sc, e_sc, l_sc, wg_ref, wa_ref, wp_ref, wm_ref, stage_sc, sem):
    c = pl.program_id(1)

    @pl.when((pl.program_id(0) == 0) & (c == 0))
    def _():
        layer = layer_ref[0]
        jobs = (_weight_jobs(w_in_hbm, layer, wg_ref, D_MODEL, 3 * D_ATTN + D_POOL, 2 * D_MODEL)
                + _weight_jobs(wa_hbm, layer, wa_ref, D_ATTN, 0, D_MODEL)
                + _weight_jobs(wp_hbm, layer, wp_ref, D_POOL, 0, D_MODEL)
                + _weight_jobs(wm_hbm, layer, wm_ref, D_MODEL, 0, D_MODEL))
        _load_weights_as_bf16(jobs, stage_sc, sem)

    even_head = lax.broadcasted_iota(jnp.int32, (GRID_W, LANES), 1) < HEAD_DIM

    def window(rr):
        r = c * MIX_ROWS + rr
        rs = jnp.clip(r - KH // 2, 0, ROWS - KH)
        return r - rs, pl.multiple_of(rs * GRID_W, GRID_W)

    def scores_softmax(rr, slot):
        delta, k0 = window(rr)
        rows = slice(rr * GRID_W, (rr + 1) * GRID_W)
        valid = valid_ref[...] > 0.5
        for p in range(N_PAIRS):
            cols = slice(p * LANES, (p + 1) * LANES)
            lhs = jnp.concatenate([qlo_ref[rows, cols], qhi_ref[rows, cols]], axis=0)
            kw = k_ref[pl.ds(k0, WIN_KEYS), cols]
            s = lax.dot_general(lhs, kw, (((1,), (1,)), ((), ())),
                                preferred_element_type=f32)
            parts = []
            for j in range(WIN_KEYS // LANES):
                sj = s[:, j * LANES:(j + 1) * LANES] + bias_ref[2 * j + KH - 1 - delta, p]
                parts.append(jnp.where(valid, sj, NEG_INF))
            m = functools.reduce(jnp.maximum, parts)
            m = jnp.max(m, axis=-1, keepdims=True)
            es = [jnp.exp(sj - m) for sj in parts]
            l_sc[slot, p] = jnp.sum(functools.reduce(lambda x, y: x + y, es), axis=-1, keepdims=True)
            for j, ej in enumerate(es):
                e_sc[slot, p, :, j * LANES:(j + 1) * LANES] = ej.astype(bf16)

    def weighted_values(rr, slot):
        _, k0 = window(rr)
        rows = slice(rr * GRID_W, (rr + 1) * GRID_W)
        for p in range(N_PAIRS):
            cols = slice(p * LANES, (p + 1) * LANES)
            vw = v_ref[pl.ds(k0, WIN_KEYS), cols]
            o = _dot(e_sc[slot, p], vw) / l_sc[slot, p]
            o = jnp.where(even_head, o[:GRID_W], o[GRID_W:])
            attn_sc[rows, cols] = o.astype(bf16)

    for it in range(MIX_ROWS + 1):
        if it < MIX_ROWS:
            scores_softmax(it, it % 2)
        if it >= 1:
            weighted_values(it - 1, (it - 1) % 2)

    part_rows = MIX_TILE // MIX_DENSE_PARTS
    for part in range(MIX_DENSE_PARTS):
        rows = slice(part * part_rows, (part + 1) * part_rows)
        h = h_ref[rows, :]
        hb = h.astype(bf16)
        ga = jax.nn.sigmoid(_dot(hb, wg_ref[:, 0:D_MODEL]) + bg_ref[:, 0:D_MODEL])
        merged = ga * _dot(attn_sc[rows, :], wa_ref[...])
        gb = jax.nn.sigmoid(_dot(hb, wg_ref[:, D_MODEL:2 * D_MODEL]) + bg_ref[:, D_MODEL:2 * D_MODEL])
        merged = merged + gb * _dot(pooled_ref[rows, :], wp_ref[...])
        z = ALPHA * h + _dot(merged.astype(bf16), wm_ref[...])
        o_ref[rows, :] = _layer_norm(z, g_ref[...], b_ref[...])


def _mix(layer, h3d, qlo, qhi, k, v, pooled, bias_tbl, valid_tbl,
         w_in, b_in, w_attn_out, w_pool_out, w_mix_out, ln_g, ln_b):
    tile = lambda width: pl.BlockSpec((None, MIX_TILE, width), lambda b, c, l: (b, c, 0))
    seq = lambda width: pl.BlockSpec((None, SEQ, width), lambda b, c, l: (b, 0, 0))
    grid_spec = pltpu.PrefetchScalarGridSpec(
        num_scalar_prefetch=1,
        grid=(BATCH, SEQ // MIX_TILE),
        in_specs=[
            tile(D_MODEL), tile(D_ATTN), tile(D_ATTN), seq(D_ATTN), seq(D_ATTN), tile(D_POOL),
            _layer_block((2 * KH - 2, N_PAIRS, 2 * GRID_W, LANES), single_buffer=True),
            pl.BlockSpec((2 * GRID_W, LANES), lambda b, c, l: (0, 0)),
            _HBM,
            _layer_block((1, 2 * D_MODEL), col_block=1),
            _HBM, _HBM, _HBM,
            _layer_block((1, D_MODEL)), _layer_block((1, D_MODEL)),
        ],
        out_specs=tile(D_MODEL),
        scratch_shapes=[
            pltpu.VMEM((MIX_TILE, D_ATTN), bf16),
            pltpu.VMEM((2, N_PAIRS, 2 * GRID_W, WIN_KEYS), bf16),
            pltpu.VMEM((2, N_PAIRS, 2 * GRID_W, 1), f32),
            pltpu.VMEM((D_MODEL, 2 * D_MODEL), bf16),
            pltpu.VMEM((D_ATTN, D_MODEL), bf16),
            pltpu.VMEM((D_POOL, D_MODEL), bf16),
            pltpu.VMEM((D_MODEL, D_MODEL), bf16),
            pltpu.VMEM((STAGE_SLOTS, STAGE_ROWS, 2 * D_MODEL), f32),
            pltpu.SemaphoreType.DMA((STAGE_SLOTS,)),
        ],
    )
    return pl.pallas_call(
        _mix_kernel,
        grid_spec=grid_spec,
        out_shape=jax.ShapeDtypeStruct((BATCH, SEQ, D_MODEL), f32),
        compiler_params=pltpu.CompilerParams(
            dimension_semantics=("arbitrary", "arbitrary"), vmem_limit_bytes=VMEM_LIMIT),
        name="mix",
    )(layer, h3d, qlo, qhi, k, v, pooled, bias_tbl, valid_tbl,
      w_in, b_in, w_attn_out, w_pool_out, w_mix_out, ln_g, ln_b)


def _ffn_kernel(layer_ref, h_ref, hprev_ref, hnext_ref, p_ref,
                w_up_hbm, cw_ref, cb_ref, wd_hbm, wpg_hbm, wpp_hbm, g_ref, b_ref,
                o_ref, hext_sc, val_sc, gate_sc, act_sc,
                wv_ref, wgt_ref, wd_ref, wpg_ref, wpp_ref, stage_sc, sem):
    i = pl.program_id(0)

    @pl.when(i == 0)
    def _():
        layer = layer_ref[0]
        jobs = (_weight_jobs(wpg_hbm, layer, wpg_ref, D_MODEL, 0, D_MODEL)
                + _weight_jobs(wpp_hbm, layer, wpp_ref, PLE_DIM, 0, D_MODEL)
                + _weight_jobs(w_up_hbm, layer, wv_ref, D_MODEL, 0, D_FF)
                + _weight_jobs(w_up_hbm, layer, wgt_ref, D_MODEL, D_FF, D_FF)
                + _weight_jobs(wd_hbm, layer, wd_ref, D_FF, 0, D_MODEL))
        _load_weights_as_bf16(jobs, stage_sc, sem)

    tiles_per_seq = SEQ // FFN_TILE
    pos = i % tiles_per_seq
    keep_prev = (pos != 0).astype(f32)
    keep_next = (pos != tiles_per_seq - 1).astype(f32)

    h = h_ref[...]
    hb = h.astype(bf16)
    main = slice(FFN_HALO, FFN_HALO + FFN_TILE)
    hext_sc[main, :] = hb
    hext_sc[0:FFN_HALO, :] = (hprev_ref[...] * keep_prev).astype(bf16)
    hext_sc[FFN_HALO + FFN_TILE:, :] = (hnext_ref[...] * keep_next).astype(bf16)

    def up(c, slot):
        cols = slice(c * FF_CHUNK, (c + 1) * FF_CHUNK)
        val_sc[slot] = _dot(hext_sc[main, :], wv_ref[:, cols])
        gate_sc[slot] = _dot(hext_sc[...], wgt_ref[:, cols])

    def activate(c, slot):
        cols = slice(c * FF_CHUNK, (c + 1) * FF_CHUNK)
        gate = gate_sc[slot]
        n = FFN_TILE + 2 * FFN_HALO
        prev = pltpu.roll(gate, 1, 0)[main]
        nxt = pltpu.roll(gate, n - 1, 0)[main]
        y = prev * cw_ref[0:1, cols] + gate[main] * cw_ref[1:2, cols] + nxt * cw_ref[2:3, cols] + cb_ref[:, cols]
        act = 0.5 * y * (1.0 + lax.erf(y * np.float32(np.sqrt(0.5))))
        act_sc[:, cols] = (act * val_sc[slot]).astype(bf16)

    for it in range(N_FF_CHUNKS + 1):
        if it < N_FF_CHUNKS:
            up(it, it % 2)
        if it >= 1:
            activate(it - 1, (it - 1) % 2)
    ple = jax.nn.sigmoid(_dot(hb, wpg_ref[...])) * _dot(p_ref[...].astype(bf16), wpp_ref[...])
    z = ALPHA * h + ple + _dot(act_sc[...], wd_ref[...])
    o_ref[...] = _layer_norm(z, g_ref[...], b_ref[...])


def _ffn(layer, h2d, p3d, w_up, conv_w, conv_b, w_down, w_ple_gate, w_ple_proj, ln_g, ln_b):
    halo_blocks = FFN_TILE // FFN_HALO
    n_halo_blocks = TOKENS // FFN_HALO
    grid_spec = pltpu.PrefetchScalarGridSpec(
        num_scalar_prefetch=1,
        grid=(TOKENS // FFN_TILE,),
        in_specs=[
            pl.BlockSpec((FFN_TILE, D_MODEL), lambda i, l: (i, 0)),
            pl.BlockSpec((FFN_HALO, D_MODEL), lambda i, l: (jnp.maximum(i * halo_blocks - 1, 0), 0)),
            pl.BlockSpec((FFN_HALO, D_MODEL),
                         lambda i, l: (jnp.minimum((i + 1) * halo_blocks, n_halo_blocks - 1), 0)),
            pl.BlockSpec((None, FFN_TILE, PLE_DIM), lambda i, l: (l[0], i, 0)),
            _HBM,
            _layer_block((3, D_FF)),
            _layer_block((1, D_FF)),
            _HBM, _HBM, _HBM,
            _layer_block((1, D_MODEL)), _layer_block((1, D_MODEL)),
        ],
        out_specs=pl.BlockSpec((FFN_TILE, D_MODEL), lambda i, l: (i, 0)),
        scratch_shapes=[
            pltpu.VMEM((FFN_TILE + 2 * FFN_HALO, D_MODEL), bf16),
            pltpu.VMEM((2, FFN_TILE, FF_CHUNK), f32),
            pltpu.VMEM((2, FFN_TILE + 2 * FFN_HALO, FF_CHUNK), f32),
            pltpu.VMEM((FFN_TILE, D_FF), bf16),
            pltpu.VMEM((D_MODEL, D_FF), bf16),
            pltpu.VMEM((D_MODEL, D_FF), bf16),
            pltpu.VMEM((D_FF, D_MODEL), bf16),
            pltpu.VMEM((D_MODEL, D_MODEL), bf16),
            pltpu.VMEM((PLE_DIM, D_MODEL), bf16),
            pltpu.VMEM((STAGE_SLOTS, STAGE_ROWS, D_FF), f32),
            pltpu.SemaphoreType.DMA((STAGE_SLOTS,)),
        ],
    )
    return pl.pallas_call(
        _ffn_kernel,
        grid_spec=grid_spec,
        out_shape=jax.ShapeDtypeStruct((TOKENS, D_MODEL), f32),
        compiler_params=pltpu.CompilerParams(
            dimension_semantics=("arbitrary",), vmem_limit_bytes=VMEM_LIMIT),
        name="ffn",
    )(layer, h2d, h2d, h2d, p3d, w_up, conv_w, conv_b, w_down, w_ple_gate, w_ple_proj, ln_g, ln_b)


def _column_geometry():
    qc = np.arange(GRID_W)[:, None]
    kc = np.arange(GRID_W)[None, :]
    col_start = np.clip(qc - KW // 2, 0, GRID_W - KW)
    valid = (kc >= col_start) & (kc < col_start + KW)
    col_off = np.clip(kc - qc, -(KW - 1), KW - 1) + KW - 1
    onehot = (col_off[..., None] == np.arange(2 * KW - 1)).astype(np.float32)
    valid_tbl = np.tile(valid.astype(np.float32), (2, 2))
    return onehot, valid_tbl


def _bias_tables(rpb):
    onehot, _ = _column_geometry()
    select = np.einsum('ij,qkc->qikjc', np.eye(2, dtype=np.float32), onehot)
    select = select.reshape(GRID_W, LANES, 2, 2 * KW - 1)
    two_rows = jnp.stack([rpb[:, :, :-1], rpb[:, :, 1:]], axis=3)
    tbl = jnp.einsum('lhrjc,qmjc->lrhqm', two_rows, jnp.asarray(select),
                     precision=lax.Precision.HIGHEST)
    return tbl.reshape(DEPTH, 2 * KH - 2, N_PAIRS, 2 * GRID_W, LANES)


def kernel(x, p, ln_in_g, ln_in_b, w_in, b_in, rpb, w_attn_out, pool_w, pool_scale, w_pool_out,
           w_mix_out, ln1_g, ln1_b, w_up, conv_w, conv_b, w_down, w_ple_gate, w_ple_proj,
           ln2_g, ln2_b):
    valid_tbl = jnp.asarray(_column_geometry()[1])
    bias_tbl = _bias_tables(rpb)
    p3d = p.reshape(DEPTH, TOKENS, PLE_DIM)
    row = lambda a: a.reshape(1, -1)
    rows = lambda a: a.reshape(DEPTH, 1, -1)
    pool_w_b = pool_w.astype(bf16)
    b_in_r, pool_scale_r, conv_b_r = rows(b_in), rows(pool_scale), rows(conv_b)
    ln1_g_r, ln1_b_r, ln2_g_r, ln2_b_r = rows(ln1_g), rows(ln1_b), rows(ln2_g), rows(ln2_b)

    h = _input_ln(x.reshape(TOKENS, D_MODEL), row(ln_in_g), row(ln_in_b))
    for i in range(DEPTH):
        layer = jnp.full((1,), i, jnp.int32)
        h3d = h.reshape(BATCH, SEQ, D_MODEL)
        qlo, qhi, k, v, pooled = _qkvu(layer, h3d, w_in, b_in_r, pool_w_b, pool_scale_r)
        h3d = _mix(layer, h3d, qlo, qhi, k, v, pooled, bias_tbl, valid_tbl,
                   w_in, b_in_r, w_attn_out, w_pool_out, w_mix_out, ln1_g_r, ln1_b_r)
        h = _ffn(layer, h3d.reshape(TOKENS, D_MODEL), p3d, w_up, conv_w, conv_b_r, w_down,
                 w_ple_gate, w_ple_proj, ln2_g_r, ln2_b_r)
    return h.reshape(BATCH, SEQ, D_MODEL)
```

```python
import functools

import numpy as np
import jax
import jax.numpy as jnp
from jax import lax
from jax.experimental import pallas as pl
from jax.experimental.pallas import tpu as pltpu

D_MODEL = 1024
BATCH = 8
SEQ = 2048
DEPTH = 4
GRID_W = 64
ROWS = SEQ // GRID_W
N_HEADS = 8
HEAD_DIM = 64
D_ATTN = N_HEADS * HEAD_DIM
N_PAIRS = N_HEADS // 2
KH = 8
KW = 16
WIN_KEYS = KH * GRID_W
POOL_WINDOWS = (2, 4, 8, 16)
D_POOL = 512
POOL_GROUP_DIM = 128
D_FF = 2816
FF_CHUNK = 256
N_FF_CHUNKS = D_FF // FF_CHUNK
PLE_DIM = 256
ALPHA = (2 * DEPTH) ** 0.25
LN_EPS = 1e-5
NEG_INF = -1e30
QK_SCALE = HEAD_DIM ** -0.5

LANES = 128
BF16_SUBLANES = 16
VMEM_LIMIT = 56 * 1024 * 1024

TOKENS = BATCH * SEQ
LN_TILE = 1024
QKVU_CHUNK = 512
POOL_CHUNK = 256
POOL_HALO = 16
MIX_ROWS = 8
MIX_TILE = MIX_ROWS * GRID_W
MIX_DENSE_PARTS = 2
FFN_TILE = 512
FFN_HALO = BF16_SUBLANES
FFN_FINAL_PARTS = 2
STAGE_ROWS = 256
STAGE_SLOTS = 3

f32 = jnp.float32
bf16 = jnp.bfloat16


def _layer_norm(xf, g, b):
    mu = jnp.mean(xf, axis=-1, keepdims=True)
    xc = xf - mu
    var = jnp.mean(xc * xc, axis=-1, keepdims=True)
    return xc * lax.rsqrt(var + LN_EPS) * g + b


def _dot(a, b):
    return jnp.dot(a, b, preferred_element_type=f32)


def _layer_block(shape, col_block=0, single_buffer=False):
    def index_map(*args):
        layer = args[-1][0]
        return (layer,) + (0,) * (len(shape) - 1) + (col_block,)
    mode = dict(pipeline_mode=pl.Buffered(1)) if single_buffer else {}
    return pl.BlockSpec((None,) + tuple(shape), index_map, **mode)


def _load_weights_as_bf16(jobs, stage_sc, sem):
    def copy(j):
        src, _, _, rows, cols = jobs[j]
        slot = j % STAGE_SLOTS
        return pltpu.make_async_copy(src, stage_sc.at[slot, pl.ds(0, rows), pl.ds(0, cols)], sem.at[slot])

    for j in range(min(STAGE_SLOTS - 1, len(jobs))):
        copy(j).start()
    for j, (_, dst, row0, rows, cols) in enumerate(jobs):
        if j + STAGE_SLOTS - 1 < len(jobs):
            copy(j + STAGE_SLOTS - 1).start()
        copy(j).wait()
        dst[row0:row0 + rows, :] = stage_sc[j % STAGE_SLOTS, 0:rows, 0:cols].astype(bf16)


def _weight_jobs(w_hbm, layer, dst, n_rows, col0, cols):
    return [(w_hbm.at[layer, pl.ds(r0, min(STAGE_ROWS, n_rows - r0)), pl.ds(col0, cols)],
             dst, r0, min(STAGE_ROWS, n_rows - r0), cols)
            for r0 in range(0, n_rows, STAGE_ROWS)]


_HBM = pl.BlockSpec(memory_space=pl.ANY)


def _ln_kernel(x_ref, g_ref, b_ref, o_ref):
    o_ref[...] = _layer_norm(x_ref[...], g_ref[...], b_ref[...])


def _input_ln(x2d, g, b):
    return pl.pallas_call(
        _ln_kernel,
        grid=(TOKENS // LN_TILE,),
        in_specs=[
            pl.BlockSpec((LN_TILE, D_MODEL), lambda i: (i, 0)),
            pl.BlockSpec((1, D_MODEL), lambda i: (0, 0)),
            pl.BlockSpec((1, D_MODEL), lambda i: (0, 0)),
        ],
        out_specs=pl.BlockSpec((LN_TILE, D_MODEL), lambda i: (i, 0)),
        out_shape=jax.ShapeDtypeStruct((TOKENS, D_MODEL), f32),
        compiler_params=pltpu.CompilerParams(
            dimension_semantics=("arbitrary",), vmem_limit_bytes=VMEM_LIMIT),
        name="input_ln",
    )(x2d, g, b)


def _qkvu_kernel(layer_ref, h_ref, w_hbm, b_ref, pw_ref, ps_ref,
                 qlo_ref, qhi_ref, k_ref, v_ref, pooled_ref,
                 upad_sc, pd_sc, w_ref, stage_sc, sem):
    n_qkvu = 3 * D_ATTN + D_POOL

    @pl.when(pl.program_id(0) == 0)
    def _():
        _load_weights_as_bf16(_weight_jobs(w_hbm, layer_ref[0], w_ref, D_MODEL, 0, n_qkvu), stage_sc, sem)

    lane = lax.broadcasted_iota(jnp.int32, (QKVU_CHUNK, D_ATTN), 1)
    even_head = (lane % LANES) < HEAD_DIM

    zeros_halo = jnp.zeros((POOL_HALO, D_POOL), f32)
    upad_sc[0:POOL_HALO, :] = zeros_halo
    upad_sc[POOL_HALO + SEQ:POOL_HALO + SEQ + POOL_HALO, :] = zeros_halo

    for c in range(SEQ // QKVU_CHUNK):
        rows = slice(c * QKVU_CHUNK, (c + 1) * QKVU_CHUNK)
        hb = h_ref[rows, :].astype(bf16)
        q = (_dot(hb, w_ref[:, 0:D_ATTN]) + b_ref[:, 0:D_ATTN]) * QK_SCALE
        qlo_ref[rows, :] = jnp.where(even_head, q, 0.0).astype(bf16)
        qhi_ref[rows, :] = jnp.where(even_head, 0.0, q).astype(bf16)
        k = _dot(hb, w_ref[:, D_ATTN:2 * D_ATTN]) + b_ref[:, D_ATTN:2 * D_ATTN]
        k_ref[rows, :] = k.astype(bf16)
        v = _dot(hb, w_ref[:, 2 * D_ATTN:3 * D_ATTN]) + b_ref[:, 2 * D_ATTN:3 * D_ATTN]
        v_ref[rows, :] = v.astype(bf16)
        u = _dot(hb, w_ref[:, 3 * D_ATTN:3 * D_ATTN + D_POOL]) + b_ref[:, 3 * D_ATTN:3 * D_ATTN + D_POOL]
        upad_sc[POOL_HALO + c * QKVU_CHUNK:POOL_HALO + (c + 1) * QKVU_CHUNK, :] = u

    for c in range(SEQ // POOL_CHUNK):
        t = lax.broadcasted_iota(jnp.int32, (POOL_CHUNK, 1), 0) + c * POOL_CHUNK
        base = POOL_HALO + c * POOL_CHUNK
        for g, w in enumerate(POOL_WINDOWS):
            half = w // 2
            cols = slice(g * POOL_GROUP_DIM, (g + 1) * POOL_GROUP_DIM)
            acc = upad_sc[base - half:base - half + POOL_CHUNK, cols]
            for j in range(-half + 1, half):
                acc = acc + upad_sc[base + j:base + j + POOL_CHUNK, cols]
            cnt = (jnp.minimum(t + half, SEQ) - jnp.maximum(t - half, 0)).astype(f32)
            centre = upad_sc[base:base + POOL_CHUNK, cols]
            pd_sc[c * POOL_CHUNK:(c + 1) * POOL_CHUNK, cols] = (acc / cnt - centre).astype(bf16)

    for g in range(len(POOL_WINDOWS)):
        cols = slice(g * POOL_GROUP_DIM, (g + 1) * POOL_GROUP_DIM)
        pooled = _dot(pd_sc[:, cols], pw_ref[g]) * ps_ref[:, cols]
        pooled_ref[:, cols] = pooled.astype(bf16)


def _qkvu(layer, h3d, w_in, b_in, pool_w, pool_scale):
    n_qkvu = 3 * D_ATTN + D_POOL
    seq_spec = lambda width: pl.BlockSpec((None, SEQ, width), lambda b, l: (b, 0, 0))
    out_sds = jax.ShapeDtypeStruct((BATCH, SEQ, D_ATTN), bf16)
    grid_spec = pltpu.PrefetchScalarGridSpec(
        num_scalar_prefetch=1,
        grid=(BATCH,),
        in_specs=[
            seq_spec(D_MODEL),
            _HBM,
            _layer_block((1, n_qkvu)),
            _layer_block((len(POOL_WINDOWS), POOL_GROUP_DIM, POOL_GROUP_DIM)),
            _layer_block((1, D_POOL)),
        ],
        out_specs=[seq_spec(D_ATTN)] * 5,
        scratch_shapes=[
            pltpu.VMEM((SEQ + 2 * POOL_HALO, D_POOL), f32),
            pltpu.VMEM((SEQ, D_POOL), bf16),
            pltpu.VMEM((D_MODEL, n_qkvu), bf16),
            pltpu.VMEM((STAGE_SLOTS, STAGE_ROWS, n_qkvu), f32),
            pltpu.SemaphoreType.DMA((STAGE_SLOTS,)),
        ],
    )
    return pl.pallas_call(
        _qkvu_kernel,
        grid_spec=grid_spec,
        out_shape=[out_sds] * 5,
        compiler_params=pltpu.CompilerParams(
            dimension_semantics=("arbitrary",), vmem_limit_bytes=VMEM_LIMIT),
        name="qkvu",
    )(layer, h3d, w_in, b_in, pool_w, pool_scale)


def _mix_kernel(layer_ref, h_ref, qlo_ref, qhi_ref, k_ref, v_ref, pooled_ref, bias_ref, valid_ref,
                w_in_hbm, bg_ref, wa_hbm, wp_hbm, wm_hbm, g_ref, b_ref,
                o_ref, attn_sc, e_sc, l_sc, wg_ref, wa_ref, wp_ref, wm_ref, stage_sc, sem):
    c = pl.program_id(1)

    @pl.when((pl.program_id(0) == 0) & (c == 0))
    def _():
        layer = layer_ref[0]
        jobs = (_weight_jobs(w_in_hbm, layer, wg_ref, D_MODEL, 3 * D_ATTN + D_POOL, 2 * D_MODEL)
                + _weight_jobs(wa_hbm, layer, wa_ref, D_ATTN, 0, D_MODEL)
                + _weight_jobs(wp_hbm, layer, wp_ref, D_POOL, 0, D_MODEL)
                + _weight_jobs(wm_hbm, layer, wm_ref, D_MODEL, 0, D_MODEL))
        _load_weights_as_bf16(jobs, stage_sc, sem)

    even_head = lax.broadcasted_iota(jnp.int32, (GRID_W, LANES), 1) < HEAD_DIM

    def window(rr):
        r = c * MIX_ROWS + rr
        rs = jnp.clip(r - KH // 2, 0, ROWS - KH)
        return r - rs, pl.multiple_of(rs * GRID_W, GRID_W)

    def scores_softmax(rr, slot):
        delta, k0 = window(rr)
        rows = slice(rr * GRID_W, (rr + 1) * GRID_W)
        valid = valid_ref[...] > 0.5
        for p in range(N_PAIRS):
            cols = slice(p * LANES, (p + 1) * LANES)
            lhs = jnp.concatenate([qlo_ref[rows, cols], qhi_ref[rows, cols]], axis=0)
            kw = k_ref[pl.ds(k0, WIN_KEYS), cols]
            s = lax.dot_general(lhs, kw, (((1,), (1,)), ((), ())),
                                preferred_element_type=f32)
            parts = []
            for j in range(WIN_KEYS // LANES):
                sj = s[:, j * LANES:(j + 1) * LANES] + bias_ref[2 * j + KH - 1 - delta, p]
                parts.append(jnp.where(valid, sj, NEG_INF))
            m = functools.reduce(jnp.maximum, parts)
            m = jnp.max(m, axis=-1, keepdims=True)
            es = [jnp.exp(sj - m) for sj in parts]
            l_sc[slot, p] = jnp.sum(functools.reduce(lambda x, y: x + y, es), axis=-1, keepdims=True)
            for j, ej in enumerate(es):
                e_sc[slot, p, :, j * LANES:(j + 1) * LANES] = ej.astype(bf16)

    def weighted_values(rr, slot):
        _, k0 = window(rr)
        rows = slice(rr * GRID_W, (rr + 1) * GRID_W)
        for p in range(N_PAIRS):
            cols = slice(p * LANES, (p + 1) * LANES)
            vw = v_ref[pl.ds(k0, WIN_KEYS), cols]
            o = _dot(e_sc[slot, p], vw) / l_sc[slot, p]
            o = jnp.where(even_head, o[:GRID_W], o[GRID_W:])
            attn_sc[rows, cols] = o.astype(bf16)

    for it in range(MIX_ROWS + 1):
        if it < MIX_ROWS:
            scores_softmax(it, it % 2)
        if it >= 1:
            weighted_values(it - 1, (it - 1) % 2)

    part_rows = MIX_TILE // MIX_DENSE_PARTS
    for part in range(MIX_DENSE_PARTS):
        rows = slice(part * part_rows, (part + 1) * part_rows)
        h = h_ref[rows, :]
        hb = h.astype(bf16)
        ga = jax.nn.sigmoid(_dot(hb, wg_ref[:, 0:D_MODEL]) + bg_ref[:, 0:D_MODEL])
        merged = ga * _dot(attn_sc[rows, :], wa_ref[...])
        gb = jax.nn.sigmoid(_dot(hb, wg_ref[:, D_MODEL:2 * D_MODEL]) + bg_ref[:, D_MODEL:2 * D_MODEL])
        merged = merged + gb * _dot(pooled_ref[rows, :], wp_ref[...])
        z = ALPHA * h + _dot(merged.astype(bf16), wm_ref[...])
        o_ref[rows, :] = _layer_norm(z, g_ref[...], b_ref[...])


def _mix(layer, h3d, qlo, qhi, k, v, pooled, bias_tbl, valid_tbl,
         w_in, b_in, w_attn_out, w_pool_out, w_mix_out, ln_g, ln_b):
    tile = lambda width: pl.BlockSpec((None, MIX_TILE, width), lambda b, c, l: (b, c, 0))
    seq = lambda width: pl.BlockSpec((None, SEQ, width), lambda b, c, l: (b, 0, 0))
    grid_spec = pltpu.PrefetchScalarGridSpec(
        num_scalar_prefetch=1,
        grid=(BATCH, SEQ // MIX_TILE),
        in_specs=[
            tile(D_MODEL), tile(D_ATTN), tile(D_ATTN), seq(D_ATTN), seq(D_ATTN), tile(D_POOL),
            _layer_block((2 * KH - 2, N_PAIRS, 2 * GRID_W, LANES), single_buffer=True),
            pl.BlockSpec((2 * GRID_W, LANES), lambda b, c, l: (0, 0)),
            _HBM,
            _layer_block((1, 2 * D_MODEL), col_block=1),
            _HBM, _HBM, _HBM,
            _layer_block((1, D_MODEL)), _layer_block((1, D_MODEL)),
        ],
        out_specs=tile(D_MODEL),
        scratch_shapes=[
            pltpu.VMEM((MIX_TILE, D_ATTN), bf16),
            pltpu.VMEM((2, N_PAIRS, 2 * GRID_W, WIN_KEYS), bf16),
            pltpu.VMEM((2, N_PAIRS, 2 * GRID_W, 1), f32),
            pltpu.VMEM((D_MODEL, 2 * D_MODEL), bf16),
            pltpu.VMEM((D_ATTN, D_MODEL), bf16),
            pltpu.VMEM((D_POOL, D_MODEL), bf16),
            pltpu.VMEM((D_MODEL, D_MODEL), bf16),
            pltpu.VMEM((STAGE_SLOTS, STAGE_ROWS, 2 * D_MODEL), f32),
            pltpu.SemaphoreType.DMA((STAGE_SLOTS,)),
        ],
    )
    return pl.pallas_call(
        _mix_kernel,
        grid_spec=grid_spec,
        out_shape=jax.ShapeDtypeStruct((BATCH, SEQ, D_MODEL), f32),
        compiler_params=pltpu.CompilerParams(
            dimension_semantics=("arbitrary", "arbitrary"), vmem_limit_bytes=VMEM_LIMIT),
        name="mix",
    )(layer, h3d, qlo, qhi, k, v, pooled, bias_tbl, valid_tbl,
      w_in, b_in, w_attn_out, w_pool_out, w_mix_out, ln_g, ln_b)


def _ffn_kernel(layer_ref, h_ref, hprev_ref, hnext_ref, p_ref,
                w_up_hbm, cw_ref, cb_ref, wd_hbm, wpg_hbm, wpp_hbm, g_ref, b_ref,
                o_ref, hext_sc, val_sc, gate_sc, act_sc,
                wv_ref, wgt_ref, wd_ref, wpg_ref, wpp_ref, stage_sc, sem):
    i = pl.program_id(0)

    @pl.when(i == 0)
    def _():
        layer = layer_ref[0]
        jobs = (_weight_jobs(wpg_hbm, layer, wpg_ref, D_MODEL, 0, D_MODEL)
                + _weight_jobs(wpp_hbm, layer, wpp_ref, PLE_DIM, 0, D_MODEL)
                + _weight_jobs(w_up_hbm, layer, wv_ref, D_MODEL, 0, D_FF)
                + _weight_jobs(w_up_hbm, layer, wgt_ref, D_MODEL, D_FF, D_FF)
                + _weight_jobs(wd_hbm, layer, wd_ref, D_FF, 0, D_MODEL))
        _load_weights_as_bf16(jobs, stage_sc, sem)

    tiles_per_seq = SEQ // FFN_TILE
    pos = i % tiles_per_seq
    keep_prev = (pos != 0).astype(f32)
    keep_next = (pos != tiles_per_seq - 1).astype(f32)

    h = h_ref[...]
    hb = h.astype(bf16)
    main = slice(FFN_HALO, FFN_HALO + FFN_TILE)
    hext_sc[main, :] = hb
    hext_sc[0:FFN_HALO, :] = (hprev_ref[...] * keep_prev).astype(bf16)
    hext_sc[FFN_HALO + FFN_TILE:, :] = (hnext_ref[...] * keep_next).astype(bf16)

    def up(c, slot):
        cols = slice(c * FF_CHUNK, (c + 1) * FF_CHUNK)
        val_sc[slot] = _dot(hext_sc[main, :], wv_ref[:, cols])
        gate_sc[slot] = _dot(hext_sc[...], wgt_ref[:, cols])

    def activate(c, slot):
        cols = slice(c * FF_CHUNK, (c + 1) * FF_CHUNK)
        gate = gate_sc[slot]
        n = FFN_TILE + 2 * FFN_HALO
        prev = pltpu.roll(gate, 1, 0)[main]
        nxt = pltpu.roll(gate, n - 1, 0)[main]
        y = prev * cw_ref[0:1, cols] + gate[main] * cw_ref[1:2, cols] + nxt * cw_ref[2:3, cols] + cb_ref[:, cols]
        act = 0.5 * y * (1.0 + lax.erf(y * np.float32(np.sqrt(0.5))))
        act_sc[:, cols] = (act * val_sc[slot]).astype(bf16)

    for it in range(N_FF_CHUNKS + 1):
        if it < N_FF_CHUNKS:
            up(it, it % 2)
        if it >= 1:
            activate(it - 1, (it - 1) % 2)
    part_rows = FFN_TILE // FFN_FINAL_PARTS
    for part in range(FFN_FINAL_PARTS):
        rows = slice(part * part_rows, (part + 1) * part_rows)
        hb_part = hext_sc[FFN_HALO + part * part_rows:FFN_HALO + (part + 1) * part_rows, :]
        ple = (jax.nn.sigmoid(_dot(hb_part, wpg_ref[...]))
               * _dot(p_ref[rows, :].astype(bf16), wpp_ref[...]))
        z = ALPHA * h_ref[rows, :] + ple + _dot(act_sc[rows, :], wd_ref[...])
        o_ref[rows, :] = _layer_norm(z, g_ref[...], b_ref[...])


def _ffn(layer, h2d, p3d, w_up, conv_w, conv_b, w_down, w_ple_gate, w_ple_proj, ln_g, ln_b):
    halo_blocks = FFN_TILE // FFN_HALO
    n_halo_blocks = TOKENS // FFN_HALO
    grid_spec = pltpu.PrefetchScalarGridSpec(
        num_scalar_prefetch=1,
        grid=(TOKENS // FFN_TILE,),
        in_specs=[
            pl.BlockSpec((FFN_TILE, D_MODEL), lambda i, l: (i, 0)),
            pl.BlockSpec((FFN_HALO, D_MODEL), lambda i, l: (jnp.maximum(i * halo_blocks - 1, 0), 0)),
            pl.BlockSpec((FFN_HALO, D_MODEL),
                         lambda i, l: (jnp.minimum((i + 1) * halo_blocks, n_halo_blocks - 1), 0)),
            pl.BlockSpec((None, FFN_TILE, PLE_DIM), lambda i, l: (l[0], i, 0)),
            _HBM,
            _layer_block((3, D_FF)),
            _layer_block((1, D_FF)),
            _HBM, _HBM, _HBM,
            _layer_block((1, D_MODEL)), _layer_block((1, D_MODEL)),
        ],
        out_specs=pl.BlockSpec((FFN_TILE, D_MODEL), lambda i, l: (i, 0)),
        scratch_shapes=[
            pltpu.VMEM((FFN_TILE + 2 * FFN_HALO, D_MODEL), bf16),
            pltpu.VMEM((2, FFN_TILE, FF_CHUNK), f32),
            pltpu.VMEM((2, FFN_TILE + 2 * FFN_HALO, FF_CHUNK), f32),
            pltpu.VMEM((FFN_TILE, D_FF), bf16),
            pltpu.VMEM((D_MODEL, D_FF), bf16),
            pltpu.VMEM((D_MODEL, D_FF), bf16),
            pltpu.VMEM((D_FF, D_MODEL), bf16),
            pltpu.VMEM((D_MODEL, D_MODEL), bf16),
            pltpu.VMEM((PLE_DIM, D_MODEL), bf16),
            pltpu.VMEM((STAGE_SLOTS, STAGE_ROWS, D_FF), f32),
            pltpu.SemaphoreType.DMA((STAGE_SLOTS,)),
        ],
    )
    return pl.pallas_call(
        _ffn_kernel,
        grid_spec=grid_spec,
        out_shape=jax.ShapeDtypeStruct((TOKENS, D_MODEL), f32),
        compiler_params=pltpu.CompilerParams(
            dimension_semantics=("arbitrary",), vmem_limit_bytes=VMEM_LIMIT),
        name="ffn",
    )(layer, h2d, h2d, h2d, p3d, w_up, conv_w, conv_b, w_down, w_ple_gate, w_ple_proj, ln_g, ln_b)


def _column_geometry():
    qc = np.arange(GRID_W)[:, None]
    kc = np.arange(GRID_W)[None, :]
    col_start = np.clip(qc - KW // 2, 0, GRID_W - KW)
    valid = (kc >= col_start) & (kc < col_start + KW)
    col_off = np.clip(kc - qc, -(KW - 1), KW - 1) + KW - 1
    onehot = (col_off[..., None] == np.arange(2 * KW - 1)).astype(np.float32)
    valid_tbl = np.tile(valid.astype(np.float32), (2, 2))
    return onehot, valid_tbl


def _bias_tables(rpb):
    onehot, _ = _column_geometry()
    select = np.einsum('ij,qkc->qikjc', np.eye(2, dtype=np.float32), onehot)
    select = select.reshape(GRID_W, LANES, 2, 2 * KW - 1)
    two_rows = jnp.stack([rpb[:, :, :-1], rpb[:, :, 1:]], axis=3)
    tbl = jnp.einsum('lhrjc,qmjc->lrhqm', two_rows, jnp.asarray(select),
                     precision=lax.Precision.HIGHEST)
    return tbl.reshape(DEPTH, 2 * KH - 2, N_PAIRS, 2 * GRID_W, LANES)


def kernel(x, p, ln_in_g, ln_in_b, w_in, b_in, rpb, w_attn_out, pool_w, pool_scale, w_pool_out,
           w_mix_out, ln1_g, ln1_b, w_up, conv_w, conv_b, w_down, w_ple_gate, w_ple_proj,
           ln2_g, ln2_b):
    valid_tbl = jnp.asarray(_column_geometry()[1])
    bias_tbl = _bias_tables(rpb)
    p3d = p.reshape(DEPTH, TOKENS, PLE_DIM)
    row = lambda a: a.reshape(1, -1)
    rows = lambda a: a.reshape(DEPTH, 1, -1)
    pool_w_b = pool_w.astype(bf16)
    b_in_r, pool_scale_r, conv_b_r = rows(b_in), rows(pool_scale), rows(conv_b)
    ln1_g_r, ln1_b_r, ln2_g_r, ln2_b_r = rows(ln1_g), rows(ln1_b), rows(ln2_g), rows(ln2_b)

    h = _input_ln(x.reshape(TOKENS, D_MODEL), row(ln_in_g), row(ln_in_b))
    for i in range(DEPTH):
        layer = jnp.full((1,), i, jnp.int32)
        h3d = h.reshape(BATCH, SEQ, D_MODEL)
        qlo, qhi, k, v, pooled = _qkvu(layer, h3d, w_in, b_in_r, pool_w_b, pool_scale_r)
        h3d = _mix(layer, h3d, qlo, qhi, k, v, pooled, bias_tbl, valid_tbl,
                   w_in, b_in_r, w_attn_out, w_pool_out, w_mix_out, ln1_g_r, ln1_b_r)
        h = _ffn(layer, h3d.reshape(TOKENS, D_MODEL), p3d, w_up, conv_w, conv_b_r, w_down,
                 w_ple_gate, w_ple_proj, ln2_g_r, ln2_b_r)
    return h.reshape(BATCH, SEQ, D_MODEL)
```

```python
import functools

import numpy as np
import jax
import jax.numpy as jnp
from jax import lax
from jax.experimental import pallas as pl
from jax.experimental.pallas import tpu as pltpu

D_MODEL = 1024
BATCH = 8
SEQ = 2048
DEPTH = 4
GRID_W = 64
ROWS = SEQ // GRID_W
N_HEADS = 8
HEAD_DIM = 64
D_ATTN = N_HEADS * HEAD_DIM
N_PAIRS = N_HEADS // 2
KH = 8
KW = 16
WIN_KEYS = KH * GRID_W
POOL_WINDOWS = (2, 4, 8, 16)
D_POOL = 512
POOL_GROUP_DIM = 128
D_FF = 2816
FF_CHUNK = 256
N_FF_CHUNKS = D_FF // FF_CHUNK
PLE_DIM = 256
ALPHA = (2 * DEPTH) ** 0.25
LN_EPS = 1e-5
NEG_INF = -1e30
QK_SCALE = HEAD_DIM ** -0.5

LANES = 128
BF16_SUBLANES = 16
VMEM_LIMIT = 56 * 1024 * 1024

TOKENS = BATCH * SEQ
LN_TILE = 1024
QKVU_CHUNK = 512
POOL_CHUNK = 256
POOL_HALO = 16
MIX_ROWS = 8
MIX_TILE = MIX_ROWS * GRID_W
MIX_DENSE_PARTS = 2
FFN_TILE = 512
FFN_HALO = BF16_SUBLANES
STAGE_ROWS = 256
STAGE_SLOTS = 3

f32 = jnp.float32
bf16 = jnp.bfloat16


def _layer_norm(xf, g, b):
    mu = jnp.mean(xf, axis=-1, keepdims=True)
    xc = xf - mu
    var = jnp.mean(xc * xc, axis=-1, keepdims=True)
    return xc * lax.rsqrt(var + LN_EPS) * g + b


def _dot(a, b):
    return jnp.dot(a, b, preferred_element_type=f32)


def _layer_block(shape, col_block=0, single_buffer=False):
    def index_map(*args):
        layer = args[-1][0]
        return (layer,) + (0,) * (len(shape) - 1) + (col_block,)
    mode = dict(pipeline_mode=pl.Buffered(1)) if single_buffer else {}
    return pl.BlockSpec((None,) + tuple(shape), index_map, **mode)


def _load_weights_as_bf16(jobs, stage_sc, sem):
    _start_weight_copies(jobs, stage_sc, sem)
    _finish_weight_copies(jobs, stage_sc, sem)


def _weight_copy(jobs, j, stage_sc, sem):
    src, _, _, rows, cols = jobs[j]
    slot = j % STAGE_SLOTS
    return pltpu.make_async_copy(src, stage_sc.at[slot, pl.ds(0, rows), pl.ds(0, cols)], sem.at[slot])


def _start_weight_copies(jobs, stage_sc, sem):
    for j in range(min(STAGE_SLOTS - 1, len(jobs))):
        _weight_copy(jobs, j, stage_sc, sem).start()


def _finish_weight_copies(jobs, stage_sc, sem):
    copy = lambda j: _weight_copy(jobs, j, stage_sc, sem)
    for j, (_, dst, row0, rows, cols) in enumerate(jobs):
        if j + STAGE_SLOTS - 1 < len(jobs):
            copy(j + STAGE_SLOTS - 1).start()
        copy(j).wait()
        dst[row0:row0 + rows, :] = stage_sc[j % STAGE_SLOTS, 0:rows, 0:cols].astype(bf16)


def _weight_jobs(w_hbm, layer, dst, n_rows, col0, cols):
    return [(w_hbm.at[layer, pl.ds(r0, min(STAGE_ROWS, n_rows - r0)), pl.ds(col0, cols)],
             dst, r0, min(STAGE_ROWS, n_rows - r0), cols)
            for r0 in range(0, n_rows, STAGE_ROWS)]


_HBM = pl.BlockSpec(memory_space=pl.ANY)


def _ln_kernel(x_ref, g_ref, b_ref, o_ref):
    o_ref[...] = _layer_norm(x_ref[...], g_ref[...], b_ref[...])


def _input_ln(x2d, g, b):
    return pl.pallas_call(
        _ln_kernel,
        grid=(TOKENS // LN_TILE,),
        in_specs=[
            pl.BlockSpec((LN_TILE, D_MODEL), lambda i: (i, 0)),
            pl.BlockSpec((1, D_MODEL), lambda i: (0, 0)),
            pl.BlockSpec((1, D_MODEL), lambda i: (0, 0)),
        ],
        out_specs=pl.BlockSpec((LN_TILE, D_MODEL), lambda i: (i, 0)),
        out_shape=jax.ShapeDtypeStruct((TOKENS, D_MODEL), f32),
        compiler_params=pltpu.CompilerParams(
            dimension_semantics=("arbitrary",), vmem_limit_bytes=VMEM_LIMIT),
        name="input_ln",
    )(x2d, g, b)


def _qkvu_kernel(layer_ref, h_ref, w_hbm, b_ref, pw_ref, ps_ref,
                 qlo_ref, qhi_ref, k_ref, v_ref, pooled_ref,
                 upad_sc, pd_sc, w_ref, stage_sc, sem):
    n_qkvu = 3 * D_ATTN + D_POOL

    @pl.when(pl.program_id(0) == 0)
    def _():
        _load_weights_as_bf16(_weight_jobs(w_hbm, layer_ref[0], w_ref, D_MODEL, 0, n_qkvu), stage_sc, sem)

    lane = lax.broadcasted_iota(jnp.int32, (QKVU_CHUNK, D_ATTN), 1)
    even_head = (lane % LANES) < HEAD_DIM

    zeros_halo = jnp.zeros((POOL_HALO, D_POOL), f32)
    upad_sc[0:POOL_HALO, :] = zeros_halo
    upad_sc[POOL_HALO + SEQ:POOL_HALO + SEQ + POOL_HALO, :] = zeros_halo

    for c in range(SEQ // QKVU_CHUNK):
        rows = slice(c * QKVU_CHUNK, (c + 1) * QKVU_CHUNK)
        hb = h_ref[rows, :].astype(bf16)
        q = (_dot(hb, w_ref[:, 0:D_ATTN]) + b_ref[:, 0:D_ATTN]) * QK_SCALE
        qlo_ref[rows, :] = jnp.where(even_head, q, 0.0).astype(bf16)
        qhi_ref[rows, :] = jnp.where(even_head, 0.0, q).astype(bf16)
        k = _dot(hb, w_ref[:, D_ATTN:2 * D_ATTN]) + b_ref[:, D_ATTN:2 * D_ATTN]
        k_ref[rows, :] = k.astype(bf16)
        v = _dot(hb, w_ref[:, 2 * D_ATTN:3 * D_ATTN]) + b_ref[:, 2 * D_ATTN:3 * D_ATTN]
        v_ref[rows, :] = v.astype(bf16)
        u = _dot(hb, w_ref[:, 3 * D_ATTN:3 * D_ATTN + D_POOL]) + b_ref[:, 3 * D_ATTN:3 * D_ATTN + D_POOL]
        upad_sc[POOL_HALO + c * QKVU_CHUNK:POOL_HALO + (c + 1) * QKVU_CHUNK, :] = u

    for c in range(SEQ // POOL_CHUNK):
        t = lax.broadcasted_iota(jnp.int32, (POOL_CHUNK, 1), 0) + c * POOL_CHUNK
        base = POOL_HALO + c * POOL_CHUNK
        for g, w in enumerate(POOL_WINDOWS):
            half = w // 2
            cols = slice(g * POOL_GROUP_DIM, (g + 1) * POOL_GROUP_DIM)
            acc = upad_sc[base - half:base - half + POOL_CHUNK, cols]
            for j in range(-half + 1, half):
                acc = acc + upad_sc[base + j:base + j + POOL_CHUNK, cols]
            cnt = (jnp.minimum(t + half, SEQ) - jnp.maximum(t - half, 0)).astype(f32)
            centre = upad_sc[base:base + POOL_CHUNK, cols]
            pd_sc[c * POOL_CHUNK:(c + 1) * POOL_CHUNK, cols] = (acc / cnt - centre).astype(bf16)

    for g in range(len(POOL_WINDOWS)):
        cols = slice(g * POOL_GROUP_DIM, (g + 1) * POOL_GROUP_DIM)
        pooled = _dot(pd_sc[:, cols], pw_ref[g]) * ps_ref[:, cols]
        pooled_ref[:, cols] = pooled.astype(bf16)


def _qkvu(layer, h3d, w_in, b_in, pool_w, pool_scale):
    n_qkvu = 3 * D_ATTN + D_POOL
    seq_spec = lambda width: pl.BlockSpec((None, SEQ, width), lambda b, l: (b, 0, 0))
    out_sds = jax.ShapeDtypeStruct((BATCH, SEQ, D_ATTN), bf16)
    grid_spec = pltpu.PrefetchScalarGridSpec(
        num_scalar_prefetch=1,
        grid=(BATCH,),
        in_specs=[
            seq_spec(D_MODEL),
            _HBM,
            _layer_block((1, n_qkvu)),
            _layer_block((len(POOL_WINDOWS), POOL_GROUP_DIM, POOL_GROUP_DIM)),
            _layer_block((1, D_POOL)),
        ],
        out_specs=[seq_spec(D_ATTN)] * 5,
        scratch_shapes=[
            pltpu.VMEM((SEQ + 2 * POOL_HALO, D_POOL), f32),
            pltpu.VMEM((SEQ, D_POOL), bf16),
            pltpu.VMEM((D_MODEL, n_qkvu), bf16),
            pltpu.VMEM((STAGE_SLOTS, STAGE_ROWS, n_qkvu), f32),
            pltpu.SemaphoreType.DMA((STAGE_SLOTS,)),
        ],
    )
    return pl.pallas_call(
        _qkvu_kernel,
        grid_spec=grid_spec,
        out_shape=[out_sds] * 5,
        compiler_params=pltpu.CompilerParams(
            dimension_semantics=("arbitrary",), vmem_limit_bytes=VMEM_LIMIT),
        name="qkvu",
    )(layer, h3d, w_in, b_in, pool_w, pool_scale)


def _mix_kernel(layer_ref, h_ref, qlo_ref, qhi_ref, k_ref, v_ref, pooled_ref, bias_ref, valid_ref,
                w_in_hbm, bg_ref, wa_hbm, wp_hbm, wm_hbm, g_ref, b_ref,
                o_ref, attn_sc, e_sc, l_sc, wg_ref, wa_ref, wp_ref, wm_ref, stage_sc, sem):
    c = pl.program_id(1)

    first_step = (pl.program_id(0) == 0) & (c == 0)

    def weight_jobs():
        layer = layer_ref[0]
        return (_weight_jobs(w_in_hbm, layer, wg_ref, D_MODEL, 3 * D_ATTN + D_POOL, 2 * D_MODEL)
                + _weight_jobs(wa_hbm, layer, wa_ref, D_ATTN, 0, D_MODEL)
                + _weight_jobs(wp_hbm, layer, wp_ref, D_POOL, 0, D_MODEL)
                + _weight_jobs(wm_hbm, layer, wm_ref, D_MODEL, 0, D_MODEL))

    @pl.when(first_step)
    def _():
        _start_weight_copies(weight_jobs(), stage_sc, sem)

    even_head = lax.broadcasted_iota(jnp.int32, (GRID_W, LANES), 1) < HEAD_DIM

    def window(rr):
        r = c * MIX_ROWS + rr
        rs = jnp.clip(r - KH // 2, 0, ROWS - KH)
        return r - rs, pl.multiple_of(rs * GRID_W, GRID_W)

    def scores_softmax(rr, slot):
        delta, k0 = window(rr)
        rows = slice(rr * GRID_W, (rr + 1) * GRID_W)
        valid = valid_ref[...] > 0.5
        for p in range(N_PAIRS):
            cols = slice(p * LANES, (p + 1) * LANES)
            lhs = jnp.concatenate([qlo_ref[rows, cols], qhi_ref[rows, cols]], axis=0)
            kw = k_ref[pl.ds(k0, WIN_KEYS), cols]
            s = lax.dot_general(lhs, kw, (((1,), (1,)), ((), ())),
                                preferred_element_type=f32)
            parts = []
            for j in range(WIN_KEYS // LANES):
                sj = s[:, j * LANES:(j + 1) * LANES] + bias_ref[2 * j + KH - 1 - delta, p]
                parts.append(jnp.where(valid, sj, NEG_INF))
            m = functools.reduce(jnp.maximum, parts)
            m = jnp.max(m, axis=-1, keepdims=True)
            es = [jnp.exp(sj - m) for sj in parts]
            l_sc[slot, p] = jnp.sum(functools.reduce(lambda x, y: x + y, es), axis=-1, keepdims=True)
            for j, ej in enumerate(es):
                e_sc[slot, p, :, j * LANES:(j + 1) * LANES] = ej.astype(bf16)

    def weighted_values(rr, slot):
        _, k0 = window(rr)
        rows = slice(rr * GRID_W, (rr + 1) * GRID_W)
        for p in range(N_PAIRS):
            cols = slice(p * LANES, (p + 1) * LANES)
            vw = v_ref[pl.ds(k0, WIN_KEYS), cols]
            o = _dot(e_sc[slot, p], vw) / l_sc[slot, p]
            o = jnp.where(even_head, o[:GRID_W], o[GRID_W:])
            attn_sc[rows, cols] = o.astype(bf16)

    for it in range(MIX_ROWS + 1):
        if it < MIX_ROWS:
            scores_softmax(it, it % 2)
        if it >= 1:
            weighted_values(it - 1, (it - 1) % 2)

    @pl.when(first_step)
    def _():
        _finish_weight_copies(weight_jobs(), stage_sc, sem)

    part_rows = MIX_TILE // MIX_DENSE_PARTS
    for part in range(MIX_DENSE_PARTS):
        rows = slice(part * part_rows, (part + 1) * part_rows)
        h = h_ref[rows, :]
        hb = h.astype(bf16)
        ga = jax.nn.sigmoid(_dot(hb, wg_ref[:, 0:D_MODEL]) + bg_ref[:, 0:D_MODEL])
        merged = ga * _dot(attn_sc[rows, :], wa_ref[...])
        gb = jax.nn.sigmoid(_dot(hb, wg_ref[:, D_MODEL:2 * D_MODEL]) + bg_ref[:, D_MODEL:2 * D_MODEL])
        merged = merged + gb * _dot(pooled_ref[rows, :], wp_ref[...])
        z = ALPHA * h + _dot(merged.astype(bf16), wm_ref[...])
        o_ref[rows, :] = _layer_norm(z, g_ref[...], b_ref[...])


def _mix(layer, h3d, qlo, qhi, k, v, pooled, bias_tbl, valid_tbl,
         w_in, b_in, w_attn_out, w_pool_out, w_mix_out, ln_g, ln_b):
    tile = lambda width: pl.BlockSpec((None, MIX_TILE, width), lambda b, c, l: (b, c, 0))
    seq = lambda width: pl.BlockSpec((None, SEQ, width), lambda b, c, l: (b, 0, 0))
    grid_spec = pltpu.PrefetchScalarGridSpec(
        num_scalar_prefetch=1,
        grid=(BATCH, SEQ // MIX_TILE),
        in_specs=[
            tile(D_MODEL), tile(D_ATTN), tile(D_ATTN), seq(D_ATTN), seq(D_ATTN), tile(D_POOL),
            _layer_block((2 * KH - 2, N_PAIRS, 2 * GRID_W, LANES), single_buffer=True),
            pl.BlockSpec((2 * GRID_W, LANES), lambda b, c, l: (0, 0)),
            _HBM,
            _layer_block((1, 2 * D_MODEL), col_block=1),
            _HBM, _HBM, _HBM,
            _layer_block((1, D_MODEL)), _layer_block((1, D_MODEL)),
        ],
        out_specs=tile(D_MODEL),
        scratch_shapes=[
            pltpu.VMEM((MIX_TILE, D_ATTN), bf16),
            pltpu.VMEM((2, N_PAIRS, 2 * GRID_W, WIN_KEYS), bf16),
            pltpu.VMEM((2, N_PAIRS, 2 * GRID_W, 1), f32),
            pltpu.VMEM((D_MODEL, 2 * D_MODEL), bf16),
            pltpu.VMEM((D_ATTN, D_MODEL), bf16),
            pltpu.VMEM((D_POOL, D_MODEL), bf16),
            pltpu.VMEM((D_MODEL, D_MODEL), bf16),
            pltpu.VMEM((STAGE_SLOTS, STAGE_ROWS, 2 * D_MODEL), f32),
            pltpu.SemaphoreType.DMA((STAGE_SLOTS,)),
        ],
    )
    return pl.pallas_call(
        _mix_kernel,
        grid_spec=grid_spec,
        out_shape=jax.ShapeDtypeStruct((BATCH, SEQ, D_MODEL), f32),
        compiler_params=pltpu.CompilerParams(
            dimension_semantics=("arbitrary", "arbitrary"), vmem_limit_bytes=VMEM_LIMIT),
        name="mix",
    )(layer, h3d, qlo, qhi, k, v, pooled, bias_tbl, valid_tbl,
      w_in, b_in, w_attn_out, w_pool_out, w_mix_out, ln_g, ln_b)


def _ffn_kernel(layer_ref, h_ref, hprev_ref, hnext_ref, p_ref,
                w_up_hbm, cw_ref, cb_ref, wd_hbm, wpg_hbm, wpp_hbm, g_ref, b_ref,
                o_ref, hext_sc, val_sc, gate_sc, act_sc,
                wv_ref, wgt_ref, wd_ref, wpg_ref, wpp_ref, stage_sc, sem):
    i = pl.program_id(0)

    @pl.when(i == 0)
    def _():
        layer = layer_ref[0]
        jobs = (_weight_jobs(wpg_hbm, layer, wpg_ref, D_MODEL, 0, D_MODEL)
                + _weight_jobs(wpp_hbm, layer, wpp_ref, PLE_DIM, 0, D_MODEL)
                + _weight_jobs(w_up_hbm, layer, wv_ref, D_MODEL, 0, D_FF)
                + _weight_jobs(w_up_hbm, layer, wgt_ref, D_MODEL, D_FF, D_FF)
                + _weight_jobs(wd_hbm, layer, wd_ref, D_FF, 0, D_MODEL))
        _load_weights_as_bf16(jobs, stage_sc, sem)

    tiles_per_seq = SEQ // FFN_TILE
    pos = i % tiles_per_seq
    keep_prev = (pos != 0).astype(f32)
    keep_next = (pos != tiles_per_seq - 1).astype(f32)

    h = h_ref[...]
    hb = h.astype(bf16)
    main = slice(FFN_HALO, FFN_HALO + FFN_TILE)
    hext_sc[main, :] = hb
    hext_sc[0:FFN_HALO, :] = (hprev_ref[...] * keep_prev).astype(bf16)
    hext_sc[FFN_HALO + FFN_TILE:, :] = (hnext_ref[...] * keep_next).astype(bf16)

    def up(c, slot):
        cols = slice(c * FF_CHUNK, (c + 1) * FF_CHUNK)
        val_sc[slot] = _dot(hext_sc[main, :], wv_ref[:, cols])
        gate_sc[slot] = _dot(hext_sc[...], wgt_ref[:, cols])

    def activate(c, slot):
        cols = slice(c * FF_CHUNK, (c + 1) * FF_CHUNK)
        gate = gate_sc[slot]
        n = FFN_TILE + 2 * FFN_HALO
        prev = pltpu.roll(gate, 1, 0)[main]
        nxt = pltpu.roll(gate, n - 1, 0)[main]
        y = prev * cw_ref[0:1, cols] + gate[main] * cw_ref[1:2, cols] + nxt * cw_ref[2:3, cols] + cb_ref[:, cols]
        act = 0.5 * y * (1.0 + lax.erf(y * np.float32(np.sqrt(0.5))))
        act_sc[:, cols] = (act * val_sc[slot]).astype(bf16)

    for it in range(N_FF_CHUNKS + 1):
        if it < N_FF_CHUNKS:
            up(it, it % 2)
        if it >= 1:
            activate(it - 1, (it - 1) % 2)
    ple = jax.nn.sigmoid(_dot(hb, wpg_ref[...])) * _dot(p_ref[...].astype(bf16), wpp_ref[...])
    z = ALPHA * h + ple + _dot(act_sc[...], wd_ref[...])
    o_ref[...] = _layer_norm(z, g_ref[...], b_ref[...])


def _ffn(layer, h2d, p3d, w_up, conv_w, conv_b, w_down, w_ple_gate, w_ple_proj, ln_g, ln_b):
    halo_blocks = FFN_TILE // FFN_HALO
    n_halo_blocks = TOKENS // FFN_HALO
    grid_spec = pltpu.PrefetchScalarGridSpec(
        num_scalar_prefetch=1,
        grid=(TOKENS // FFN_TILE,),
        in_specs=[
            pl.BlockSpec((FFN_TILE, D_MODEL), lambda i, l: (i, 0)),
            pl.BlockSpec((FFN_HALO, D_MODEL), lambda i, l: (jnp.maximum(i * halo_blocks - 1, 0), 0)),
            pl.BlockSpec((FFN_HALO, D_MODEL),
                         lambda i, l: (jnp.minimum((i + 1) * halo_blocks, n_halo_blocks - 1), 0)),
            pl.BlockSpec((None, FFN_TILE, PLE_DIM), lambda i, l: (l[0], i, 0)),
            _HBM,
            _layer_block((3, D_FF)),
            _layer_block((1, D_FF)),
            _HBM, _HBM, _HBM,
            _layer_block((1, D_MODEL)), _layer_block((1, D_MODEL)),
        ],
        out_specs=pl.BlockSpec((FFN_TILE, D_MODEL), lambda i, l: (i, 0)),
        scratch_shapes=[
            pltpu.VMEM((FFN_TILE + 2 * FFN_HALO, D_MODEL), bf16),
            pltpu.VMEM((2, FFN_TILE, FF_CHUNK), f32),
            pltpu.VMEM((2, FFN_TILE + 2 * FFN_HALO, FF_CHUNK), f32),
            pltpu.VMEM((FFN_TILE, D_FF), bf16),
            pltpu.VMEM((D_MODEL, D_FF), bf16),
            pltpu.VMEM((D_MODEL, D_FF), bf16),
            pltpu.VMEM((D_FF, D_MODEL), bf16),
            pltpu.VMEM((D_MODEL, D_MODEL), bf16),
            pltpu.VMEM((PLE_DIM, D_MODEL), bf16),
            pltpu.VMEM((STAGE_SLOTS, STAGE_ROWS, D_FF), f32),
            pltpu.SemaphoreType.DMA((STAGE_SLOTS,)),
        ],
    )
    return pl.pallas_call(
        _ffn_kernel,
        grid_spec=grid_spec,
        out_shape=jax.ShapeDtypeStruct((TOKENS, D_MODEL), f32),
        compiler_params=pltpu.CompilerParams(
            dimension_semantics=("arbitrary",), vmem_limit_bytes=VMEM_LIMIT),
        name="ffn",
    )(layer, h2d, h2d, h2d, p3d, w_up, conv_w, conv_b, w_down, w_ple_gate, w_ple_proj, ln_g, ln_b)


def _column_geometry():
    qc = np.arange(GRID_W)[:, None]
    kc = np.arange(GRID_W)[None, :]
    col_start = np.clip(qc - KW // 2, 0, GRID_W - KW)
    valid = (kc >= col_start) & (kc < col_start + KW)
    col_off = np.clip(kc - qc, -(KW - 1), KW - 1) + KW - 1
    onehot = (col_off[..., None] == np.arange(2 * KW - 1)).astype(np.float32)
    valid_tbl = np.tile(valid.astype(np.float32), (2, 2))
    return onehot, valid_tbl


def _bias_tables(rpb):
    onehot, _ = _column_geometry()
    select = np.einsum('ij,qkc->qikjc', np.eye(2, dtype=np.float32), onehot)
    select = select.reshape(GRID_W, LANES, 2, 2 * KW - 1)
    two_rows = jnp.stack([rpb[:, :, :-1], rpb[:, :, 1:]], axis=3)
    tbl = jnp.einsum('lhrjc,qmjc->lrhqm', two_rows, jnp.asarray(select),
                     precision=lax.Precision.HIGHEST)
    return tbl.reshape(DEPTH, 2 * KH - 2, N_PAIRS, 2 * GRID_W, LANES)


def kernel(x, p, ln_in_g, ln_in_b, w_in, b_in, rpb, w_attn_out, pool_w, pool_scale, w_pool_out,
           w_mix_out, ln1_g, ln1_b, w_up, conv_w, conv_b, w_down, w_ple_gate, w_ple_proj,
           ln2_g, ln2_b):
    valid_tbl = jnp.asarray(_column_geometry()[1])
    bias_tbl = _bias_tables(rpb)
    p3d = p.reshape(DEPTH, TOKENS, PLE_DIM)
    row = lambda a: a.reshape(1, -1)
    rows = lambda a: a.reshape(DEPTH, 1, -1)
    pool_w_b = pool_w.astype(bf16)
    b_in_r, pool_scale_r, conv_b_r = rows(b_in), rows(pool_scale), rows(conv_b)
    ln1_g_r, ln1_b_r, ln2_g_r, ln2_b_r = rows(ln1_g), rows(ln1_b), rows(ln2_g), rows(ln2_b)

    h = _input_ln(x.reshape(TOKENS, D_MODEL), row(ln_in_g), row(ln_in_b))
    for i in range(DEPTH):
        layer = jnp.full((1,), i, jnp.int32)
        h3d = h.reshape(BATCH, SEQ, D_MODEL)
        qlo, qhi, k, v, pooled = _qkvu(layer, h3d, w_in, b_in_r, pool_w_b, pool_scale_r)
        h3d = _mix(layer, h3d, qlo, qhi, k, v, pooled, bias_tbl, valid_tbl,
                   w_in, b_in_r, w_attn_out, w_pool_out, w_mix_out, ln1_g_r, ln1_b_r)
        h = _ffn(layer, h3d.reshape(TOKENS, D_MODEL), p3d, w_up, conv_w, conv_b_r, w_down,
                 w_ple_gate, w_ple_proj, ln2_g_r, ln2_b_r)
    return h.reshape(BATCH, SEQ, D_MODEL)
```
